```python
import math
import jax, jax.numpy as jnp
from jax import lax
import numpy as np

D_MODEL = 1024
BATCH = 16
SEQ = 4096
DEPTH = 4

N_EVEN = (DEPTH + 1) // 2
N_ODD = DEPTH // 2
D_FF = 2816
MIX_WIDTH = D_MODEL
HALF = MIX_WIDTH // 2
CONV_WIDTH = 4
RG_WIDTH = HALF
RG_BLOCKS = 8
RG_BLK = RG_WIDTH // RG_BLOCKS
RG_C = 8.0
ML_HEADS = 4
ML_DH = HALF // ML_HEADS
ML_CHUNK = 64
ML_NORM_EPS = 1e-6
RK_WIDTH = HALF
RK_HEADS = 8
RK_DH = RK_WIDTH // RK_HEADS
RK_DECAY_LORA = 32
RK_A_LORA = 32
RK_G_LORA = 64
RK_NORM_EPS = 64e-5
GLA_HEADS = 4
GLA_DK = 64
GLA_DV = HALF // GLA_HEADS
GLA_GATE_LORA = 16
GLA_TAU = 16.0
GLA_CHUNK = 64
GLA_NORM_EPS = 1e-5
DEEPNORM_ALPHA = (2.0 * DEPTH) ** 0.25
DEEPNORM_BETA = (8.0 * DEPTH) ** -0.25
LN_EPS = 1e-5

EVEN_SIZES = (RG_WIDTH, RG_WIDTH, 2 * HALF, HALF, HALF, 2 * ML_HEADS)
EVEN_IN = sum(EVEN_SIZES)
RK_SIZES = (RK_WIDTH, RK_WIDTH, RK_WIDTH, RK_DECAY_LORA, RK_A_LORA, RK_G_LORA)
RK_IN = sum(RK_SIZES)
GLA_SIZES = (GLA_HEADS * GLA_DK, GLA_HEADS * GLA_DK, HALF, GLA_GATE_LORA, HALF)
GLA_IN = sum(GLA_SIZES)
ODD_IN = RK_IN + GLA_IN

kernel_name = "hybrid_rglru_mlstm_rwkv7_gla_macaron_deepnorm"


def split_cols(p, sizes):
    return jnp.split(p, [int(s) for s in np.cumsum(sizes)[:-1]], axis=-1)


def layer_norm(x, g, b):
    xf = x.astype(jnp.float32)
    mu = jnp.mean(xf, -1, keepdims=True)
    var = jnp.mean(jnp.square(xf - mu), -1, keepdims=True)
    return ((xf - mu) * lax.rsqrt(var + LN_EPS) * g + b).astype(x.dtype)


def head_norm(h, g, b, eps):
    mu = jnp.mean(h, -1, keepdims=True)
    var = jnp.mean(jnp.square(h - mu), -1, keepdims=True)
    hn = ((h - mu) * lax.rsqrt(var + eps)).reshape(h.shape[0], h.shape[1], -1)
    hn = hn * g
    if b is not None:
        hn = hn + b
    return hn


def swiglu_ffn(x, wi, wo):
    gate, up = jnp.split(x @ wi, 2, axis=-1)
    return (jax.nn.silu(gate) * up) @ wo


def causal_depthwise_conv(t, w, b):
    S = t.shape[1]
    width = w.shape[0]
    tp = jnp.pad(t, ((0, 0), (width - 1, 0), (0, 0)))
    out = b
    for j in range(width):
        out = out + w[j] * tp[:, j:j + S]
    return out


def token_shift(t, mu):
    prev = jnp.pad(t, ((0, 0), (1, 0), (0, 0)))[:, :-1]
    return t + mu * (prev - t)


def linear_scan(a, b):
    def combine(l, r):
        a_l, b_l = l
        a_r, b_r = r
        return a_l * a_r, a_r * b_l + b_r
    _, h = lax.associative_scan(combine, (a, b), axis=1)
    return h


def to_chunks(t, c):
    B, S = t.shape[:2]
    t = t.reshape(B, S // c, c, *t.shape[2:])
    return jnp.moveaxis(jnp.moveaxis(t, 1, 0), 3, 2)


def from_chunks(t):
    t = jnp.moveaxis(jnp.moveaxis(t, 2, 3), 0, 1)
    return t.reshape(t.shape[0], t.shape[1] * t.shape[2], *t.shape[3:])


def rg_lru_mixer(xr, xg, conv_w, conv_b, wa, wx, ba, bx, lam):
    B, S, W = xr.shape
    u = causal_depthwise_conv(xr, conv_w, conv_b)
    ub = u.reshape(B, S, RG_BLOCKS, RG_BLK)
    r = jax.nn.sigmoid(jnp.einsum('bsnc,ncd->bsnd', ub, wa).reshape(B, S, W) + ba)
    i = jax.nn.sigmoid(jnp.einsum('bsnc,ncd->bsnd', ub, wx).reshape(B, S, W) + bx)
    log_a = -RG_C * r * jax.nn.softplus(-lam)
    a = jnp.exp(log_a)
    b = jnp.sqrt(-jnp.expm1(2.0 * log_a)) * (i * u)
    h = linear_scan(a, b)
    return h * jax.nn.gelu(xg)


def mlstm_mixer(qk, v, o, if_pre, conv_w, conv_b, i_bias, f_bias, norm_g):
    B, S, _ = v.shape
    qk = jax.nn.silu(causal_depthwise_conv(qk, conv_w, conv_b))
    q, k = jnp.split(qk, 2, axis=-1)
    q = q.reshape(B, S, ML_HEADS, ML_DH) * (ML_DH ** -0.5)
    k = k.reshape(B, S, ML_HEADS, ML_DH)
    v = v.reshape(B, S, ML_HEADS, ML_DH)
    i_pre = if_pre[..., :ML_HEADS] + i_bias
    log_f = jax.nn.log_sigmoid(if_pre[..., ML_HEADS:] + f_bias)
    causal = jnp.tril(jnp.ones((ML_CHUNK, ML_CHUNK), bool))

    def step(carry, inp):
        c_st, n_st, m = carry
        qc, kc, vc, ic, fc = inp
        bcum = jnp.cumsum(fc, axis=-1)
        d = jnp.where(causal, bcum[..., :, None] - bcum[..., None, :] + ic[..., None, :], -jnp.inf)
        inter = bcum + m[..., None]
        m_t = jnp.maximum(inter, jnp.max(d, -1))
        w_intra = jnp.exp(d - m_t[..., None])
        w_inter = jnp.exp(inter - m_t)
        s = jnp.einsum('bhid,bhjd->bhij', qc, kc) * w_intra
        num = jnp.einsum('bhij,bhje->bhie', s, vc) + w_inter[..., None] * jnp.einsum('bhid,bhde->bhie', qc, c_st)
        den = jnp.sum(s, -1) + w_inter * jnp.einsum('bhid,bhd->bhi', qc, n_st)
        h = num / jnp.maximum(jnp.abs(den), jnp.exp(-m_t))[..., None]
        b_last = bcum[..., -1]
        g = b_last[..., None] - bcum + ic
        m_new = jnp.maximum(b_last + m, jnp.max(g, -1))
        wk = jnp.exp(g - m_new[..., None])
        decay = jnp.exp(b_last + m - m_new)
        c_st = decay[..., None, None] * c_st + jnp.einsum('bhc,bhcd,bhce->bhde', wk, kc, vc)
        n_st = decay[..., None] * n_st + jnp.einsum('bhc,bhcd->bhd', wk, kc)
        return (c_st, n_st, m_new), h

    init = (jnp.zeros((B, ML_HEADS, ML_DH, ML_DH), jnp.float32),
            jnp.zeros((B, ML_HEADS, ML_DH), jnp.float32),
            jnp.zeros((B, ML_HEADS), jnp.float32))
    xs = (to_chunks(q, ML_CHUNK), to_chunks(k, ML_CHUNK), to_chunks(v, ML_CHUNK),
          to_chunks(i_pre, ML_CHUNK), to_chunks(log_f, ML_CHUNK))
    _, h = lax.scan(step, init, xs)
    h = from_chunks(h)
    return head_norm(h, norm_g, None, ML_NORM_EPS) * jax.nn.sigmoid(o)


def rwkv7_mixer(p, mu, w0, wB, a0, aB, gB, k_k, k_a, r_k, ln_g, ln_b):
    B, S, _ = p.shape
    p = token_shift(p, mu)
    r, k, v, wd, ad, gd = split_cols(p, RK_SIZES)
    w_raw = -jax.nn.softplus(-(w0 + jnp.tanh(wd) @ wB)) - 0.5
    decay = jnp.exp(-jnp.exp(w_raw))
    a = jax.nn.sigmoid(a0 + ad @ aB)
    g = jax.nn.sigmoid(gd) @ gB
    heads = lambda t: t.reshape(B, S, RK_HEADS, RK_DH)
    kk = heads(k * k_k)
    kk = kk / jnp.maximum(jnp.sqrt(jnp.sum(kk * kk, -1, keepdims=True)), 1e-12)
    k = k * (1.0 + (a - 1.0) * k_a)
    r, k, v, decay, a = heads(r), heads(k), heads(v), heads(decay), heads(a)

    def step(state, inp):
        r_t, w_t, k_t, v_t, kk_t, a_t = inp
        sa = jnp.einsum('bhij,bhj->bhi', state, -kk_t)
        state = (state * w_t[:, :, None, :] + sa[..., None] * (kk_t * a_t)[:, :, None, :]
                 + v_t[..., None] * k_t[:, :, None, :])
        return state, jnp.einsum('bhij,bhj->bhi', state, r_t)

    tm = lambda t: jnp.moveaxis(t, 1, 0)
    init = jnp.zeros((B, RK_HEADS, RK_DH, RK_DH), jnp.float32)
    _, y = lax.scan(step, init, (tm(r), tm(decay), tm(k), tm(v), tm(kk), tm(a)))
    y = head_norm(jnp.moveaxis(y, 0, 1), ln_g, ln_b, RK_NORM_EPS)
    bonus = (jnp.sum(r * k * r_k, -1, keepdims=True) * v).reshape(B, S, RK_WIDTH)
    return (y + bonus) * g


def gla_mixer(q, k, v, gd, og, gB, gb, norm_g):
    B, S, _ = v.shape
    log_alpha = jax.nn.log_sigmoid(gd @ gB + gb) / GLA_TAU
    q = to_chunks(q.reshape(B, S, GLA_HEADS, GLA_DK) * (GLA_DK ** -0.5), GLA_CHUNK)
    k = to_chunks(k.reshape(B, S, GLA_HEADS, GLA_DK), GLA_CHUNK)
    v = to_chunks(v.reshape(B, S, GLA_HEADS, GLA_DV), GLA_CHUNK)
    la = to_chunks(log_alpha.reshape(B, S, GLA_HEADS, GLA_DK), GLA_CHUNK)
    bcum = jnp.cumsum(la, axis=3)
    q_dec = q * jnp.exp(bcum)
    k_inv = k * jnp.exp(-bcum)
    causal = jnp.tril(jnp.ones((GLA_CHUNK, GLA_CHUNK), bool))
    attn = jnp.where(causal, jnp.einsum('nbhid,nbhjd->nbhij', q_dec, k_inv), 0.0)
    o = jnp.einsum('nbhij,nbhje->nbhie', attn, v)
    b_last = bcum[..., -1, :]
    kv = jnp.einsum('nbhcd,nbhce->nbhde', k * jnp.exp(b_last[..., None, :] - bcum), v)

    def step(state, inp):
        dec, kv_c = inp
        return dec[..., None] * state + kv_c, state

    init = jnp.zeros((B, GLA_HEADS, GLA_DK, GLA_DV), jnp.float32)
    _, s_prev = lax.scan(step, init, (jnp.exp(b_last), kv))
    o = o + jnp.einsum('nbhcd,nbhde->nbhce', q_dec, s_prev)
    o = from_chunks(o)
    return head_norm(o, norm_g, None, GLA_NORM_EPS) * jax.nn.silu(og)


def even_mixer(x, w_in, w_out, rg_conv_w, rg_conv_b, rg_wa, rg_wx, rg_ba, rg_bx, rg_lambda,
               ml_conv_w, ml_conv_b, ml_i_bias, ml_f_bias, ml_norm_g):
    p = (x @ w_in).astype(jnp.float32)
    rg_x, rg_g, ml_qk, ml_v, ml_o, ml_if = split_cols(p, EVEN_SIZES)
    y_rg = rg_lru_mixer(rg_x, rg_g, rg_conv_w, rg_conv_b, rg_wa, rg_wx, rg_ba, rg_bx, rg_lambda)
    y_ml = mlstm_mixer(ml_qk, ml_v, ml_o, ml_if, ml_conv_w, ml_conv_b, ml_i_bias, ml_f_bias, ml_norm_g)
    return jnp.concatenate([y_rg, y_ml], axis=-1).astype(x.dtype) @ w_out


def odd_mixer(x, w_in, w_out, rk_mu, rk_w0, rk_wB, rk_a0, rk_aB, rk_gB, rk_k_k, rk_k_a, rk_r_k,
              rk_ln_g, rk_ln_b, gla_gB, gla_gb, gla_norm_g):
    p = (x @ w_in).astype(jnp.float32)
    p_rk, p_gla = p[..., :RK_IN], p[..., RK_IN:]
    y_rk = rwkv7_mixer(p_rk, rk_mu, rk_w0, rk_wB, rk_a0, rk_aB, rk_gB, rk_k_k, rk_k_a, rk_r_k, rk_ln_g, rk_ln_b)
    gq, gk, gv, ggd, gog = split_cols(p_gla, GLA_SIZES)
    y_gla = gla_mixer(gq, gk, gv, ggd, gog, gla_gB, gla_gb, gla_norm_g)
    return jnp.concatenate([y_rk, y_gla], axis=-1).astype(x.dtype) @ w_out


def setup_inputs(seed: int = 0) -> dict:
    key = jax.random.key(seed)
    ks = iter(jax.random.split(key, 64))
    nrm = lambda shape, scale: scale * jax.random.normal(next(ks), shape, jnp.float32)
    unif = lambda shape, lo, hi: jax.random.uniform(next(ks), shape, jnp.float32, lo, hi)
    NE, NO = N_EVEN, N_ODD
    beta = DEEPNORM_BETA
    s_rg = unif((NE, RG_WIDTH), 0.9, 0.999) ** (1.0 / RG_C)
    return {
        "x": nrm((BATCH, SEQ, D_MODEL), 1.0),
        "ffn1_wi": nrm((DEPTH, D_MODEL, 2 * D_FF), D_MODEL ** -0.5),
        "ffn1_wo": nrm((DEPTH, D_FF, D_MODEL), beta * D_FF ** -0.5),
        "ffn2_wi": nrm((DEPTH, D_MODEL, 2 * D_FF), D_MODEL ** -0.5),
        "ffn2_wo": nrm((DEPTH, D_FF, D_MODEL), beta * D_FF ** -0.5),
        "ln_g": 1.0 + nrm((DEPTH, 3, D_MODEL), 0.02),
        "ln_b": nrm((DEPTH, 3, D_MODEL), 0.02),
        "ev_w_in": nrm((NE, D_MODEL, EVEN_IN), D_MODEL ** -0.5),
        "ev_w_out": nrm((NE, MIX_WIDTH, D_MODEL), beta * MIX_WIDTH ** -0.5),
        "rg_conv_w": nrm((NE, CONV_WIDTH, RG_WIDTH), CONV_WIDTH ** -0.5),
        "rg_conv_b": nrm((NE, RG_WIDTH), 0.02),
        "rg_wa": nrm((NE, RG_BLOCKS, RG_BLK, RG_BLK), RG_BLK ** -0.5),
        "rg_wx": nrm((NE, RG_BLOCKS, RG_BLK, RG_BLK), RG_BLK ** -0.5),
        "rg_ba": nrm((NE, RG_WIDTH), 0.02),
        "rg_bx": nrm((NE, RG_WIDTH), 0.02),
        "rg_lambda": jnp.log(s_rg) - jnp.log1p(-s_rg),
        "ml_conv_w": nrm((NE, CONV_WIDTH, 2 * HALF), CONV_WIDTH ** -0.5),
        "ml_conv_b": nrm((NE, 2 * HALF), 0.02),
        "ml_i_bias": nrm((NE, ML_HEADS), 0.1),
        "ml_f_bias": jnp.linspace(3.0, 6.0, ML_HEADS)[None, :] + nrm((NE, ML_HEADS), 0.1),
        "ml_norm_g": 1.0 + nrm((NE, HALF), 0.02),
        "od_w_in": nrm((NO, D_MODEL, ODD_IN), D_MODEL ** -0.5),
        "od_w_out": nrm((NO, MIX_WIDTH, D_MODEL), beta * MIX_WIDTH ** -0.5),
        "rk_mu": unif((NO, RK_IN), 0.0, 1.0),
        "rk_w0": jnp.linspace(-6.5, -1.5, RK_WIDTH)[None, :] + nrm((NO, RK_WIDTH), 0.1),
        "rk_wB": nrm((NO, RK_DECAY_LORA, RK_WIDTH), 0.1 * RK_DECAY_LORA ** -0.5),
        "rk_a0": nrm((NO, RK_WIDTH), 0.1),
        "rk_aB": nrm((NO, RK_A_LORA, RK_WIDTH), 0.1 * RK_A_LORA ** -0.5),
        "rk_gB": nrm((NO, RK_G_LORA, RK_WIDTH), RK_G_LORA ** -0.5),
        "rk_k_k": 0.85 + nrm((NO, RK_WIDTH), 0.02),
        "rk_k_a": 1.0 + nrm((NO, RK_WIDTH), 0.02),
        "rk_r_k": nrm((NO, RK_HEADS, RK_DH), 0.1),
        "rk_ln_g": 1.0 + nrm((NO, RK_WIDTH), 0.02),
        "rk_ln_b": nrm((NO, RK_WIDTH), 0.02),
        "gla_gB": nrm((NO, GLA_GATE_LORA, GLA_HEADS * GLA_DK), GLA_GATE_LORA ** -0.5),
        "gla_gb": nrm((NO, GLA_HEADS * GLA_DK), 0.1),
        "gla_norm_g": 1.0 + nrm((NO, HALF), 0.02),
    }


def reference(x, ffn1_wi, ffn1_wo, ffn2_wi, ffn2_wo, ln_g, ln_b,
              ev_w_in, ev_w_out, rg_conv_w, rg_conv_b, rg_wa, rg_wx, rg_ba, rg_bx, rg_lambda,
              ml_conv_w, ml_conv_b, ml_i_bias, ml_f_bias, ml_norm_g,
              od_w_in, od_w_out, rk_mu, rk_w0, rk_wB, rk_a0, rk_aB, rk_gB, rk_k_k, rk_k_a, rk_r_k,
              rk_ln_g, rk_ln_b, gla_gB, gla_gb, gla_norm_g):
    alpha = DEEPNORM_ALPHA
    for l in range(DEPTH):
        x = layer_norm(alpha * x + 0.5 * swiglu_ffn(x, ffn1_wi[l], ffn1_wo[l]), ln_g[l, 0], ln_b[l, 0])
        if l % 2 == 0:
            e = l // 2
            mix = even_mixer(x, ev_w_in[e], ev_w_out[e], rg_conv_w[e], rg_conv_b[e], rg_wa[e], rg_wx[e],
                             rg_ba[e], rg_bx[e], rg_lambda[e], ml_conv_w[e], ml_conv_b[e],
                             ml_i_bias[e], ml_f_bias[e], ml_norm_g[e])
        else:
            o = l // 2
            mix = odd_mixer(x, od_w_in[o], od_w_out[o], rk_mu[o], rk_w0[o], rk_wB[o], rk_a0[o], rk_aB[o],
                            rk_gB[o], rk_k_k[o], rk_k_a[o], rk_r_k[o], rk_ln_g[o], rk_ln_b[o],
                            gla_gB[o], gla_gb[o], gla_norm_g[o])
        x = layer_norm(alpha * x + mix, ln_g[l, 1], ln_b[l, 1])
        x = layer_norm(alpha * x + 0.5 * swiglu_ffn(x, ffn2_wi[l], ffn2_wo[l]), ln_g[l, 2], ln_b[l, 2])
    return x
```

```python
import functools
import math

import jax
import jax.numpy as jnp
from jax import lax
from jax.experimental import pallas as pl
from jax.experimental.pallas import tpu as pltpu

F32 = jnp.float32
BF = jnp.bfloat16

D_MODEL = 1024
DEPTH = 4
D_FF = 2816
HALF = 512
CONV_WIDTH = 4
RG_BLOCKS = 8
RG_BLK = HALF // RG_BLOCKS
RG_C = 8.0
ML_HEADS = 4
ML_DH = HALF // ML_HEADS
ML_NORM_EPS = 1e-6
RK_HEADS = 8
RK_DH = HALF // RK_HEADS
RK_DECAY_LORA = 32
RK_A_LORA = 32
RK_G_LORA = 64
RK_NORM_EPS = 64e-5
GLA_HEADS = 4
GLA_DK = 64
GLA_DV = HALF // GLA_HEADS
GLA_GATE_LORA = 16
GLA_TAU = 16.0
GLA_NORM_EPS = 1e-5
DEEPNORM_ALPHA = (2.0 * DEPTH) ** 0.25
LN_EPS = 1e-5

LANES = 128
CHUNK = 64
CARRY_ROWS = 8
VMEM_LIMIT = 56 * 1024 * 1024


def _dot(a, b):
    return jnp.dot(a, b, preferred_element_type=F32)


def _dot_nt(a, b):
    return lax.dot_general(a, b, (((1,), (1,)), ((), ())), preferred_element_type=F32)


def _bdot(a, b):
    return _dot(a.astype(BF), b.astype(BF))


def _bdot_nt(a, b):
    return _dot_nt(a.astype(BF), b.astype(BF))


def _layer_norm(r, g, b):
    mu = jnp.mean(r, axis=-1, keepdims=True)
    c = r - mu
    var = jnp.mean(c * c, axis=-1, keepdims=True)
    return c * lax.rsqrt(var + LN_EPS) * g + b


def _softplus(z):
    return jnp.maximum(z, 0.0) + jnp.log1p(jnp.exp(-jnp.abs(z)))


def _log_sigmoid(z):
    return -_softplus(-z)


def _split3_left_dot(l_bf, x):
    hi = x.astype(BF)
    r1 = x - hi.astype(F32)
    mid = r1.astype(BF)
    lo = (r1 - mid.astype(F32)).astype(BF)
    return _dot(l_bf, hi) + _dot(l_bf, mid) + _dot(l_bf, lo)


def _chunk_tril(n):
    r = lax.broadcasted_iota(jnp.int32, (n, n), 0)
    c = lax.broadcasted_iota(jnp.int32, (n, n), 1)
    same = jnp.bitwise_and(r, -CHUNK) == jnp.bitwise_and(c, -CHUNK)
    return jnp.where((c <= r) & same, 1.0, 0.0).astype(BF)


def _shift_rows(x, d, fill):
    rolled = pltpu.roll(x, d, axis=0)
    row = lax.broadcasted_iota(jnp.int32, x.shape, 0)
    return jnp.where(row >= d, rolled, fill)


def _linear_scan_rows(a, b):
    n = a.shape[0]
    d = 1
    while d < n:
        a_s = _shift_rows(a, d, 1.0)
        b_s = _shift_rows(b, d, 0.0)
        b = a * b_s + b
        a = a * a_s
        d *= 2
    return a, b


def _whole(shape):
    return pl.BlockSpec(shape, lambda *_: (0,) * len(shape))


def _params(n_axes):
    return pltpu.CompilerParams(
        dimension_semantics=("arbitrary",) * n_axes, vmem_limit_bytes=VMEM_LIMIT)


def _pad_heads(w, n_heads, dh):
    lead = w.shape[:-1]
    w = w.reshape(*lead, n_heads, dh)
    w = jnp.pad(w, [(0, 0)] * len(lead) + [(0, 0), (0, LANES - dh)])
    return w.reshape(*lead, n_heads * LANES)


def _pad_head_rows(w, n_heads, dh):
    return _pad_heads(w.T, n_heads, dh).T


def _row(v):
    return v.reshape(1, -1).astype(F32)


def _ffn_kernel(x_ref, wi_ref, wo_ref, g_ref, b_ref, o_ref):
    x = x_ref[...]
    xb = x.astype(BF)
    gate = _dot(xb, wi_ref[:, :D_FF])
    up = _dot(xb, wi_ref[:, D_FF:])
    act = (gate * jax.nn.sigmoid(gate) * up).astype(BF)
    y = _dot(act, wo_ref[...])
    o_ref[...] = _layer_norm(DEEPNORM_ALPHA * x + 0.5 * y, g_ref[...], b_ref[...])


def _ffn_ln(x2, wi, wo, g, b, tm):
    t = x2.shape[0]
    return pl.pallas_call(
        _ffn_kernel,
        grid=(t // tm,),
        in_specs=[pl.BlockSpec((tm, D_MODEL), lambda i: (i, 0)),
                  _whole(wi.shape), _whole(wo.shape), _whole((1, D_MODEL)), _whole((1, D_MODEL))],
        out_specs=pl.BlockSpec((tm, D_MODEL), lambda i: (i, 0)),
        out_shape=jax.ShapeDtypeStruct((t, D_MODEL), F32),
        compiler_params=_params(1),
        name="ffn_ln",
    )(x2, wi, wo, _row(g), _row(b))


def _outproj_kernel(ya_ref, yb_ref, x_ref, wa_ref, wb_ref, g_ref, b_ref, o_ref):
    mix = _dot(ya_ref[...], wa_ref[...]) + _dot(yb_ref[...], wb_ref[...])
    o_ref[...] = _layer_norm(DEEPNORM_ALPHA * x_ref[...] + mix, g_ref[...], b_ref[...])


def _outproj_ln(ya, yb, x2, wa, wb, g, b, tm):
    t = x2.shape[0]
    ka, kb = ya.shape[1], yb.shape[1]
    return pl.pallas_call(
        _outproj_kernel,
        grid=(t // tm,),
        in_specs=[pl.BlockSpec((tm, ka), lambda i: (i, 0)),
                  pl.BlockSpec((tm, kb), lambda i: (i, 0)),
                  pl.BlockSpec((tm, D_MODEL), lambda i: (i, 0)),
                  _whole(wa.shape), _whole(wb.shape), _whole((1, D_MODEL)), _whole((1, D_MODEL))],
        out_specs=pl.BlockSpec((tm, D_MODEL), lambda i: (i, 0)),
        out_shape=jax.ShapeDtypeStruct((t, D_MODEL), F32),
        compiler_params=_params(1),
        name="outproj_ln",
    )(ya, yb, x2, wa, wb, _row(g), _row(b))


def _seq_call(kernel, x3, weights, out_width, scratch, ts, name):
    bsz, seq, _ = x3.shape
    return pl.pallas_call(
        kernel,
        grid=(bsz, seq // ts),
        in_specs=[pl.BlockSpec((None, ts, D_MODEL), lambda b, s: (b, s, 0))]
                 + [_whole(w.shape) for w in weights],
        out_specs=pl.BlockSpec((None, ts, out_width), lambda b, s: (b, s, 0)),
        out_shape=jax.ShapeDtypeStruct((bsz, seq, out_width), BF),
        scratch_shapes=scratch,
        compiler_params=_params(2),
        name=name,
    )(x3, *weights)


def _causal_conv(buf, cur, cw_ref, cb_ref, ts):
    buf[CARRY_ROWS:CARRY_ROWS + ts, :] = cur
    out = cb_ref[...] + cw_ref[3:4, :] * cur
    for j in range(1, CONV_WIDTH):
        out = out + cw_ref[3 - j:4 - j, :] * buf[CARRY_ROWS - j:CARRY_ROWS - j + ts, :]
    buf[0:CARRY_ROWS, :] = buf[ts:ts + CARRY_ROWS, :]
    return out


def _rglru_kernel(x_ref, wx_ref, wg_ref, cw_ref, cb_ref, wa_ref, wi_ref, ba_ref, bi_ref, lam_ref,
                  o_ref, buf, hcar, *, ts):
    @pl.when(pl.program_id(1) == 0)
    def _():
        buf[0:CARRY_ROWS, :] = jnp.zeros((CARRY_ROWS, HALF), F32)
        hcar[...] = jnp.zeros_like(hcar)

    xb = x_ref[...].astype(BF)
    xr = _dot(xb, wx_ref[...])
    xg = _dot(xb, wg_ref[...])
    u = _causal_conv(buf, xr, cw_ref, cb_ref, ts)
    ub = u.astype(BF)
    r = jax.nn.sigmoid(_dot(ub, wa_ref[...]) + ba_ref[...])
    i = jax.nn.sigmoid(_dot(ub, wi_ref[...]) + bi_ref[...])
    log_a = -RG_C * r * _softplus(-lam_ref[...])
    a = jnp.exp(log_a)
    bb = jnp.sqrt(1.0 - a * a) * (i * u)
    a_cum, h0 = _linear_scan_rows(a, bb)
    h = h0 + a_cum * hcar[0:1, :]
    hcar[...] = jnp.broadcast_to(h[ts - 1:ts, :], hcar.shape)
    o_ref[...] = (h * jax.nn.gelu(xg, approximate=True)).astype(o_ref.dtype)


def _block_diag(w):
    n, c, d = w.shape
    eye = jnp.eye(n, dtype=w.dtype)
    return (eye[:, None, :, None] * w[:, :, None, :]).reshape(n * c, n * d)


def _rglru(x3, w_x, w_g, conv_w, conv_b, wa, wx, ba, bx, lam, ts):
    weights = [w_x.astype(BF), w_g.astype(BF), conv_w.astype(F32), _row(conv_b),
               _block_diag(wa).astype(BF), _block_diag(wx).astype(BF), _row(ba), _row(bx), _row(lam)]
    scratch = [pltpu.VMEM((ts + CARRY_ROWS, HALF), F32), pltpu.VMEM((8, HALF), F32)]
    return _seq_call(functools.partial(_rglru_kernel, ts=ts), x3, weights, HALF, scratch, ts, "rglru")


def _mlstm_kernel(x_ref, wqk_ref, wv_ref, wo_ref, wif_ref, cw_ref, cb_ref, gb_ref, ng_ref,
                  o_ref, buf, q_s, k_s, v_s, og_s, comb_s, c_s, m_s, *, ts):
    @pl.when(pl.program_id(1) == 0)
    def _():
        buf[0:CARRY_ROWS, :] = jnp.zeros((CARRY_ROWS, 2 * HALF), F32)
        c_s[...] = jnp.zeros_like(c_s)
        m_s[...] = jnp.zeros_like(m_s)

    xb = x_ref[...].astype(BF)
    qk = _causal_conv(buf, _dot(xb, wqk_ref[...]), cw_ref, cb_ref, ts)
    qk = qk * jax.nn.sigmoid(qk)
    q_s[...] = qk[:, :HALF] * (ML_DH ** -0.5)
    k_s[...] = qk[:, HALF:]
    v_s[...] = _dot(xb, wv_ref[...])
    og_s[...] = jax.nn.sigmoid(_dot(xb, wo_ref[...]))
    z = _dot(xb, wif_ref[...]) + gb_ref[...]
    fcum = _split3_left_dot(_chunk_tril(ts), _log_sigmoid(z))
    lane = lax.broadcasted_iota(jnp.int32, z.shape, 1)
    comb_s[...] = jnp.where(lane < ML_HEADS, z, fcum)

    ri = lax.broadcasted_iota(jnp.int32, (CHUNK, CHUNK), 0)
    ci = lax.broadcasted_iota(jnp.int32, (CHUNK, CHUNK), 1)
    causal = ci <= ri
    e0 = jnp.where(lax.broadcasted_iota(jnp.int32, (CHUNK, LANES), 1) == 0, 1.0, 0.0)

    def chunk(c, carry):
        rows = pl.ds(pl.multiple_of(c * CHUNK, CHUNK), CHUNK)
        comb = comb_s[rows, :]
        comb_t = comb.T
        for h in range(ML_HEADS):
            lanes = slice(h * ML_DH, (h + 1) * ML_DH)
            q = q_s[rows, lanes]
            k = k_s[rows, lanes]
            v = v_s[rows, lanes]
            i_col = comb[:, h:h + 1]
            f_col = comb[:, ML_HEADS + h:ML_HEADS + h + 1]
            i_row = comb_t[h:h + 1, :]
            f_row = comb_t[ML_HEADS + h:ML_HEADS + h + 1, :]
            m = m_s[h:h + 1, 0:1]
            d = jnp.where(causal, f_col - f_row + i_row, -jnp.inf)
            inter = f_col + m
            m_t = jnp.maximum(inter, jnp.max(d, axis=-1, keepdims=True))
            w_intra = jnp.exp(d - m_t)
            w_inter = jnp.exp(inter - m_t)
            s = _bdot_nt(q, k) * w_intra
            vaug = jnp.concatenate([v, e0], axis=1).astype(BF)
            caug = c_s[h]
            na = _dot(s.astype(BF), vaug) + w_inter * _bdot(q, caug)
            num = na[:, :ML_DH]
            den = na[:, ML_DH:ML_DH + 1]
            hh = num / jnp.maximum(jnp.abs(den), jnp.exp(-m_t))
            f_last = f_col[CHUNK - 1:CHUNK, :]
            g = f_last - f_col + i_col
            m_new = jnp.maximum(f_last + m, jnp.max(g, axis=0, keepdims=True))
            wk = jnp.exp(g - m_new)
            decay = jnp.exp(f_last + m - m_new)
            c_s[h] = decay * caug + _dot((k * wk).T.astype(BF), vaug)
            m_s[h:h + 1, :] = jnp.broadcast_to(m_new, (1, LANES))
            mu = jnp.mean(hh, axis=-1, keepdims=True)
            hc = hh - mu
            var = jnp.mean(hc * hc, axis=-1, keepdims=True)
            hn = hc * lax.rsqrt(var + ML_NORM_EPS) * ng_ref[:, lanes]
            o_ref[rows, lanes] = (hn * og_s[rows, lanes]).astype(o_ref.dtype)
        return carry

    lax.fori_loop(0, ts // CHUNK, chunk, 0)


def _mlstm(x3, w_qk, w_v, w_o, w_if, conv_w, conv_b, i_bias, f_bias, norm_g, ts):
    wif = jnp.pad(w_if, ((0, 0), (0, LANES - 2 * ML_HEADS)))
    gate_bias = jnp.pad(jnp.concatenate([i_bias, f_bias]), (0, LANES - 2 * ML_HEADS))
    weights = [w_qk.astype(BF), w_v.astype(BF), w_o.astype(BF), wif.astype(BF),
               conv_w.astype(F32), _row(conv_b), _row(gate_bias), _row(norm_g)]
    scratch = [pltpu.VMEM((ts + CARRY_ROWS, 2 * HALF), F32),
               pltpu.VMEM((ts, HALF), F32), pltpu.VMEM((ts, HALF), F32),
               pltpu.VMEM((ts, HALF), F32), pltpu.VMEM((ts, HALF), F32),
               pltpu.VMEM((ts, LANES), F32),
               pltpu.VMEM((ML_HEADS, ML_DH, 2 * LANES), F32),
               pltpu.VMEM((8, LANES), F32)]
    return _seq_call(functools.partial(_mlstm_kernel, ts=ts), x3, weights, HALF, scratch, ts, "mlstm")


def _gla_kernel(x_ref, wq_ref, wk_ref, wv_ref, wgd_ref, wog_ref, gB_ref, gb_ref, ng_ref,
                o_ref, q_s, k_s, v_s, og_s, bc_s, st_s, *, ts):
    @pl.when(pl.program_id(1) == 0)
    def _():
        st_s[...] = jnp.zeros_like(st_s)

    xb = x_ref[...].astype(BF)
    q_s[...] = _dot(xb, wq_ref[...]) * (GLA_DK ** -0.5)
    k_s[...] = _dot(xb, wk_ref[...])
    v_s[...] = _dot(xb, wv_ref[...])
    og = _dot(xb, wog_ref[...])
    og_s[...] = og * jax.nn.sigmoid(og)
    gd = _dot(xb, wgd_ref[...])
    log_alpha = _log_sigmoid(_bdot(gd, gB_ref[...]) + gb_ref[...]) / GLA_TAU
    bc_s[...] = _split3_left_dot(_chunk_tril(ts), log_alpha)

    ri = lax.broadcasted_iota(jnp.int32, (CHUNK, CHUNK), 0)
    ci = lax.broadcasted_iota(jnp.int32, (CHUNK, CHUNK), 1)
    causal = ci <= ri

    def chunk(c, carry):
        rows = pl.ds(pl.multiple_of(c * CHUNK, CHUNK), CHUNK)
        for h in range(GLA_HEADS):
            lanes = slice(h * LANES, (h + 1) * LANES)
            bc = bc_s[rows, lanes]
            q_dec = q_s[rows, lanes] * jnp.exp(bc)
            k = k_s[rows, lanes]
            k_inv = k * jnp.exp(-bc)
            v = v_s[rows, lanes]
            attn = jnp.where(causal, _bdot_nt(q_dec, k_inv), 0.0)
            st = st_s[h]
            o = _bdot(attn, v) + _bdot_nt(q_dec, st)
            b_last = bc[CHUNK - 1:CHUNK, :]
            k_dec = (k * jnp.exp(b_last - bc)).astype(BF)
            st_s[h] = st * jnp.exp(b_last) + _dot(v.T.astype(BF), k_dec)
            mu = jnp.mean(o, axis=-1, keepdims=True)
            oc = o - mu
            var = jnp.mean(oc * oc, axis=-1, keepdims=True)
            on = oc * lax.rsqrt(var + GLA_NORM_EPS) * ng_ref[:, lanes]
            o_ref[rows, lanes] = (on * og_s[rows, lanes]).astype(o_ref.dtype)
        return carry

    lax.fori_loop(0, ts // CHUNK, chunk, 0)


def _gla(x3, w_q, w_k, w_v, w_gd, w_og, gB, gb, norm_g, ts):
    weights = [_pad_heads(w_q, GLA_HEADS, GLA_DK).astype(BF), _pad_heads(w_k, GLA_HEADS, GLA_DK).astype(BF),
               w_v.astype(BF), jnp.pad(w_gd, ((0, 0), (0, LANES - GLA_GATE_LORA))).astype(BF), w_og.astype(BF),
               jnp.pad(_pad_heads(gB, GLA_HEADS, GLA_DK), ((0, LANES - GLA_GATE_LORA), (0, 0))).astype(BF),
               _row(_pad_heads(gb, GLA_HEADS, GLA_DK)), _row(norm_g)]
    wide = GLA_HEADS * LANES
    scratch = [pltpu.VMEM((ts, wide), F32), pltpu.VMEM((ts, wide), F32),
               pltpu.VMEM((ts, HALF), F32), pltpu.VMEM((ts, HALF), F32),
               pltpu.VMEM((ts, wide), F32),
               pltpu.VMEM((GLA_HEADS, GLA_DV, LANES), F32)]
    return _seq_call(functools.partial(_gla_kernel, ts=ts), x3, weights, HALF, scratch, ts, "gla")


RK_WIDE = RK_HEADS * LANES
RK_PROJ = 3 * RK_WIDE + LANES


def _rwkv_kernel(x_ref, w_ref, mu_ref, w0_ref, wB_ref, a0_ref, aB_ref, gB_ref, kk_ref, ka_ref, rk_ref,
                 lg_ref, lb_ref, o_ref, buf, r_s, k_s, v_s, kap_s, b_s, lw_s, cum_s, g_s, bon_s, h_s, *, ts):
    @pl.when(pl.program_id(1) == 0)
    def _():
        buf[0:CARRY_ROWS, :] = jnp.zeros((CARRY_ROWS, RK_PROJ), F32)
        h_s[...] = jnp.zeros_like(h_s)

    xb = x_ref[...].astype(BF)
    p = _dot(xb, w_ref[...])
    buf[CARRY_ROWS:CARRY_ROWS + ts, :] = p
    prev = buf[CARRY_ROWS - 1:CARRY_ROWS - 1 + ts, :]
    buf[0:CARRY_ROWS, :] = buf[ts:ts + CARRY_ROWS, :]
    p = p + mu_ref[...] * (prev - p)
    r = p[:, 0:RK_WIDE]
    k = p[:, RK_WIDE:2 * RK_WIDE]
    v = p[:, 2 * RK_WIDE:3 * RK_WIDE]
    lora = p[:, 3 * RK_WIDE:]
    w_raw = -_softplus(-(w0_ref[...] + _bdot(jnp.tanh(lora), wB_ref[...]))) - 0.5
    lw = -jnp.exp(w_raw)
    a = jax.nn.sigmoid(a0_ref[...] + _bdot(lora, aB_ref[...]))
    g_s[...] = _bdot(jax.nn.sigmoid(lora), gB_ref[...])
    kk = k * kk_ref[...]
    k2 = k * (1.0 + (a - 1.0) * ka_ref[...])
    rkk = r * k2 * rk_ref[...]
    for h in range(RK_HEADS):
        lanes = slice(h * LANES, (h + 1) * LANES)
        kkh = kk[:, lanes]
        nrm = jnp.sqrt(jnp.sum(kkh * kkh, axis=-1, keepdims=True))
        kap = kkh / jnp.maximum(nrm, 1e-12)
        kap_s[:, lanes] = kap
        b_s[:, lanes] = kap * a[:, lanes]
        bon_s[:, lanes] = jnp.sum(rkk[:, lanes], axis=-1, keepdims=True) * v[:, lanes]
    r_s[...] = r
    k_s[...] = k2
    v_s[...] = v
    lw_s[...] = lw
    cum_s[...] = _split3_left_dot(_chunk_tril(ts), lw)

    ri = lax.broadcasted_iota(jnp.int32, (CHUNK, CHUNK), 0)
    ci = lax.broadcasted_iota(jnp.int32, (CHUNK, CHUNK), 1)
    strict = ci < ri
    incl = ci <= ri
    eye_c = jnp.where(ci == ri, 1.0, 0.0)
    rl = lax.broadcasted_iota(jnp.int32, (LANES, LANES), 0)
    cl = lax.broadcasted_iota(jnp.int32, (LANES, LANES), 1)
    eye_l = rl == cl
    real = lax.broadcasted_iota(jnp.int32, (CHUNK, LANES), 1) < RK_DH

    def chunk(c, carry):
        rows = pl.ds(pl.multiple_of(c * CHUNK, CHUNK), CHUNK)
        for h in range(RK_HEADS):
            lanes = slice(h * LANES, (h + 1) * LANES)
            cum = cum_s[rows, lanes]
            gam = jnp.exp(cum)
            ginv = jnp.exp(-cum)
            gex = jnp.exp(cum - lw_s[rows, lanes])
            c_last = cum[CHUNK - 1:CHUNK, :]
            gend = jnp.exp(c_last - cum)
            k2h = k_s[rows, lanes]
            bh = b_s[rows, lanes]
            vh = v_s[rows, lanes]
            rt = r_s[rows, lanes] * gam
            kt = kap_s[rows, lanes] * gex
            lhs = jnp.concatenate([kt, rt], axis=0)
            x1 = _bdot_nt(lhs, k2h * ginv)
            x2 = _bdot_nt(lhs, bh * ginv)
            a_kk = jnp.where(strict, x1[:CHUNK], 0.0)
            a_rk = jnp.where(incl, x1[CHUNK:], 0.0)
            a_kb = jnp.where(strict, x2[:CHUNK], 0.0)
            a_rb = jnp.where(incl, x2[CHUNK:], 0.0)
            xp = -a_kb
            t_inv = eye_c + xp
            xp = _bdot(xp, xp)
            for _ in range(4):
                z = _bdot(jnp.concatenate([t_inv, xp], axis=0), xp)
                t_inv = t_inv + z[:CHUNK]
                xp = z[CHUNK:]
            t_inv = t_inv + _bdot(t_inv, xp)
            kp = _bdot(t_inv, kt)
            w2 = _bdot(t_inv, a_kk)
            lhs2 = jnp.concatenate([a_rb, (bh * gend).T], axis=0)
            m1 = _bdot(lhs2, kp)
            m2 = _bdot(lhs2, w2)
            r_eff = rt - m1[:CHUNK]
            g_eff = jnp.where(eye_l, jnp.exp(c_last), 0.0) - m1[CHUNK:]
            p_eff = a_rk - m2[:CHUNK]
            q_eff = (k2h * gend).T - m2[CHUNK:]
            o1 = _bdot(jnp.concatenate([r_eff, g_eff], axis=0), h_s[h])
            o2 = _bdot(jnp.concatenate([p_eff, q_eff], axis=0), vh)
            y = o1[:CHUNK] + o2[:CHUNK]
            h_s[h] = o1[CHUNK:] + o2[CHUNK:]
            mu = jnp.sum(y, axis=-1, keepdims=True) * (1.0 / RK_DH)
            yc = jnp.where(real, y - mu, 0.0)
            var = jnp.sum(yc * yc, axis=-1, keepdims=True) * (1.0 / RK_DH)
            yn = yc * lax.rsqrt(var + RK_NORM_EPS) * lg_ref[:, lanes] + lb_ref[:, lanes]
            o_ref[rows, lanes] = ((yn + bon_s[rows, lanes]) * g_s[rows, lanes]).astype(o_ref.dtype)
        return carry

    lax.fori_loop(0, ts // CHUNK, chunk, 0)


def _rwkv(x3, w_rk, mu, w0, wB, a0, aB, gB, k_k, k_a, r_k, ln_g, ln_b, ts):
    ph = lambda t: _pad_heads(t, RK_HEADS, RK_DH)
    w_r, w_k, w_v, w_l = (w_rk[:, :HALF], w_rk[:, HALF:2 * HALF], w_rk[:, 2 * HALF:3 * HALF], w_rk[:, 3 * HALF:])
    w_all = jnp.concatenate([ph(w_r), ph(w_k), ph(w_v), w_l], axis=1).astype(BF)
    mu_all = jnp.concatenate([ph(mu[:HALF]), ph(mu[HALF:2 * HALF]), ph(mu[2 * HALF:3 * HALF]), mu[3 * HALF:]])
    d0, d1 = RK_DECAY_LORA, RK_DECAY_LORA + RK_A_LORA
    wB_p = jnp.zeros((LANES, RK_WIDE), F32).at[0:d0].set(ph(wB))
    aB_p = jnp.zeros((LANES, RK_WIDE), F32).at[d0:d1].set(ph(aB))
    gB_p = jnp.zeros((LANES, RK_WIDE), F32).at[d1:].set(ph(gB))
    weights = [w_all, _row(mu_all), _row(ph(w0)), wB_p.astype(BF), _row(ph(a0)), aB_p.astype(BF), gB_p.astype(BF),
               _row(ph(k_k)), _row(ph(k_a)), _row(ph(r_k.reshape(-1))), _row(ph(ln_g)), _row(ph(ln_b))]
    wide = pltpu.VMEM((ts, RK_WIDE), F32)
    scratch = [pltpu.VMEM((ts + CARRY_ROWS, RK_PROJ), F32)] + [wide] * 9 + [pltpu.VMEM((RK_HEADS, LANES, LANES), F32)]
    return _seq_call(functools.partial(_rwkv_kernel, ts=ts), x3, weights, RK_WIDE, scratch, ts, "rwkv7")


def _tile(n, pref):
    t = min(n, pref)
    while n % t:
        t //= 2
    return t


def kernel(x, ffn1_wi, ffn1_wo, ffn2_wi, ffn2_wo, ln_g, ln_b, ev_w_in, ev_w_out, rg_conv_w, rg_conv_b, rg_wa, rg_wx, rg_ba, rg_bx, rg_lambda, ml_conv_w, ml_conv_b, ml_i_bias, ml_f_bias, ml_norm_g, od_w_in, od_w_out, rk_mu, rk_w0, rk_wB, rk_a0, rk_aB, rk_gB, rk_k_k, rk_k_a, rk_r_k, rk_ln_g, rk_ln_b, gla_gB, gla_gb, gla_norm_g):
    bsz, seq, d = x.shape
    assert d == D_MODEL and seq % CHUNK == 0
    t = bsz * seq
    tm = _tile(t, 256)
    ts = _tile(seq, 256)
    rk_in = 3 * HALF + RK_DECAY_LORA + RK_A_LORA + RK_G_LORA
    x2 = x.reshape(t, d)
    for l in range(DEPTH):
        x2 = _ffn_ln(x2, ffn1_wi[l].astype(BF), ffn1_wo[l].astype(BF), ln_g[l, 0], ln_b[l, 0], tm)
        x3 = x2.reshape(bsz, seq, d)
        if l % 2 == 0:
            e = l // 2
            w = ev_w_in[e]
            ya = _rglru(x3, w[:, 0:HALF], w[:, HALF:2 * HALF], rg_conv_w[e], rg_conv_b[e], rg_wa[e], rg_wx[e],
                        rg_ba[e], rg_bx[e], rg_lambda[e], ts)
            yb = _mlstm(x3, w[:, 2 * HALF:4 * HALF], w[:, 4 * HALF:5 * HALF], w[:, 5 * HALF:6 * HALF],
                        w[:, 6 * HALF:], ml_conv_w[e], ml_conv_b[e], ml_i_bias[e], ml_f_bias[e], ml_norm_g[e], ts)
            wa = ev_w_out[e][:HALF].astype(BF)
            wb = ev_w_out[e][HALF:].astype(BF)
        else:
            o = l // 2
            w = od_w_in[o]
            ya = _rwkv(x3, w[:, :rk_in], rk_mu[o], rk_w0[o], rk_wB[o], rk_a0[o], rk_aB[o], rk_gB[o], rk_k_k[o],
                       rk_k_a[o], rk_r_k[o], rk_ln_g[o], rk_ln_b[o], ts)
            wg = w[:, rk_in:]
            q0, k0, v0, g0 = GLA_HEADS * GLA_DK, 2 * GLA_HEADS * GLA_DK, 2 * GLA_HEADS * GLA_DK + HALF, \
                2 * GLA_HEADS * GLA_DK + HALF + GLA_GATE_LORA
            yb = _gla(x3, wg[:, :q0], wg[:, q0:k0], wg[:, k0:v0], wg[:, v0:g0], wg[:, g0:],
                      gla_gB[o], gla_gb[o], gla_norm_g[o], ts)
            wa = _pad_head_rows(od_w_out[o][:HALF], RK_HEADS, RK_DH).astype(BF)
            wb = od_w_out[o][HALF:].astype(BF)
        x2 = _outproj_ln(ya.reshape(t, -1), yb.reshape(t, -1), x2, wa, wb, ln_g[l, 1], ln_b[l, 1], tm)
        x2 = _ffn_ln(x2, ffn2_wi[l].astype(BF), ffn2_wo[l].astype(BF), ln_g[l, 2], ln_b[l, 2], tm)
    return x2.reshape(bsz, seq, d)
```

```python
import functools
import math

import jax
import jax.numpy as jnp
from jax import lax
from jax.experimental import pallas as pl
from jax.experimental.pallas import tpu as pltpu

F32 = jnp.float32
BF = jnp.bfloat16

D_MODEL = 1024
DEPTH = 4
D_FF = 2816
HALF = 512
CONV_WIDTH = 4
RG_BLOCKS = 8
RG_BLK = HALF // RG_BLOCKS
RG_C = 8.0
ML_HEADS = 4
ML_DH = HALF // ML_HEADS
ML_NORM_EPS = 1e-6
RK_HEADS = 8
RK_DH = HALF // RK_HEADS
RK_DECAY_LORA = 32
RK_A_LORA = 32
RK_G_LORA = 64
RK_NORM_EPS = 64e-5
GLA_HEADS = 4
GLA_DK = 64
GLA_DV = HALF // GLA_HEADS
GLA_GATE_LORA = 16
GLA_TAU = 16.0
GLA_NORM_EPS = 1e-5
DEEPNORM_ALPHA = (2.0 * DEPTH) ** 0.25
LN_EPS = 1e-5

LANES = 128
CHUNK = 64
CARRY_ROWS = 8
SEQS_PER_BLOCK = 2
ROWS_PER_BLOCK = 256
VMEM_LIMIT = 56 * 1024 * 1024


def _dot(a, b):
    return jnp.dot(a, b, preferred_element_type=F32)


def _dot_nt(a, b):
    return lax.dot_general(a, b, (((1,), (1,)), ((), ())), preferred_element_type=F32)


def _bdot(a, b):
    return _dot(a.astype(BF), b.astype(BF))


def _bdot_nt(a, b):
    return _dot_nt(a.astype(BF), b.astype(BF))


def _layer_norm(r, g, b):
    mu = jnp.mean(r, axis=-1, keepdims=True)
    c = r - mu
    var = jnp.mean(c * c, axis=-1, keepdims=True)
    return c * lax.rsqrt(var + LN_EPS) * g + b


def _softplus(z):
    return jnp.maximum(z, 0.0) + jnp.log1p(jnp.exp(-jnp.abs(z)))


def _log_sigmoid(z):
    return -_softplus(-z)


def _split3_left_dot(l_bf, x):
    hi = x.astype(BF)
    r1 = x - hi.astype(F32)
    mid = r1.astype(BF)
    lo = (r1 - mid.astype(F32)).astype(BF)
    return _dot(l_bf, hi) + _dot(l_bf, mid) + _dot(l_bf, lo)


def _split3_right_dot(x, r_bf):
    hi = x.astype(BF)
    r1 = x - hi.astype(F32)
    mid = r1.astype(BF)
    lo = (r1 - mid.astype(F32)).astype(BF)
    return _dot(hi, r_bf) + _dot(mid, r_bf) + _dot(lo, r_bf)


def _chunk_tril(n):
    r = lax.broadcasted_iota(jnp.int32, (n, n), 0)
    c = lax.broadcasted_iota(jnp.int32, (n, n), 1)
    same = jnp.bitwise_and(r, -CHUNK) == jnp.bitwise_and(c, -CHUNK)
    return jnp.where((c <= r) & same, 1.0, 0.0).astype(BF)


def _shift_rows(x, d, fill):
    rolled = pltpu.roll(x, d, axis=0)
    row = lax.broadcasted_iota(jnp.int32, x.shape, 0)
    return jnp.where(row >= d, rolled, fill)


def _linear_scan_rows(a, b):
    n = a.shape[0]
    d = 1
    while d < n:
        a_s = _shift_rows(a, d, 1.0)
        b_s = _shift_rows(b, d, 0.0)
        b = a * b_s + b
        a = a * a_s
        d *= 2
    return a, b


def _whole(shape):
    return pl.BlockSpec(shape, lambda *_: (0,) * len(shape), pipeline_mode=pl.Buffered(1))


def _params(n_axes):
    return pltpu.CompilerParams(
        dimension_semantics=("arbitrary",) * n_axes, vmem_limit_bytes=VMEM_LIMIT)


def _pad_heads(w, n_heads, dh):
    lead = w.shape[:-1]
    w = w.reshape(*lead, n_heads, dh)
    w = jnp.pad(w, [(0, 0)] * len(lead) + [(0, 0), (0, LANES - dh)])
    return w.reshape(*lead, n_heads * LANES)


def _pad_head_rows(w, n_heads, dh):
    return _pad_heads(w.T, n_heads, dh).T


def _row(v):
    return v.reshape(1, -1).astype(F32)


def _ffn_kernel(x_ref, wi_ref, wo_ref, g_ref, b_ref, o_ref):
    x = x_ref[...]
    xb = x.astype(BF)
    gate = _dot(xb, wi_ref[:, :D_FF])
    up = _dot(xb, wi_ref[:, D_FF:])
    act = (gate * jax.nn.sigmoid(gate) * up).astype(BF)
    y = _dot(act, wo_ref[...])
    o_ref[...] = _layer_norm(DEEPNORM_ALPHA * x + 0.5 * y, g_ref[...], b_ref[...])


def _ffn_ln(x2, wi, wo, g, b, tm):
    t = x2.shape[0]
    return pl.pallas_call(
        _ffn_kernel,
        grid=(t // tm,),
        in_specs=[pl.BlockSpec((tm, D_MODEL), lambda i: (i, 0)),
                  _whole(wi.shape), _whole(wo.shape), _whole((1, D_MODEL)), _whole((1, D_MODEL))],
        out_specs=pl.BlockSpec((tm, D_MODEL), lambda i: (i, 0)),
        out_shape=jax.ShapeDtypeStruct((t, D_MODEL), F32),
        compiler_params=_params(1),
        name="ffn_ln",
    )(x2, wi, wo, _row(g), _row(b))


def _outproj_kernel(ya_ref, yb_ref, x_ref, wa_ref, wb_ref, g_ref, b_ref, o_ref):
    mix = _dot(ya_ref[...], wa_ref[...]) + _dot(yb_ref[...], wb_ref[...])
    o_ref[...] = _layer_norm(DEEPNORM_ALPHA * x_ref[...] + mix, g_ref[...], b_ref[...])


def _outproj_ln(ya, yb, x2, wa, wb, g, b, tm):
    t = x2.shape[0]
    ka, kb = ya.shape[1], yb.shape[1]
    return pl.pallas_call(
        _outproj_kernel,
        grid=(t // tm,),
        in_specs=[pl.BlockSpec((tm, ka), lambda i: (i, 0)),
                  pl.BlockSpec((tm, kb), lambda i: (i, 0)),
                  pl.BlockSpec((tm, D_MODEL), lambda i: (i, 0)),
                  _whole(wa.shape), _whole(wb.shape), _whole((1, D_MODEL)), _whole((1, D_MODEL))],
        out_specs=pl.BlockSpec((tm, D_MODEL), lambda i: (i, 0)),
        out_shape=jax.ShapeDtypeStruct((t, D_MODEL), F32),
        compiler_params=_params(1),
        name="outproj_ln",
    )(ya, yb, x2, wa, wb, _row(g), _row(b))


def _seq_call(kernel, x3, weights, out_width, scratch, nb, ts, name):
    bsz, seq, _ = x3.shape
    return pl.pallas_call(
        kernel,
        grid=(bsz // nb, seq // ts),
        in_specs=[pl.BlockSpec((nb, ts, D_MODEL), lambda b, s: (b, s, 0))]
                 + [_whole(w.shape) for w in weights],
        out_specs=pl.BlockSpec((nb, ts, out_width), lambda b, s: (b, s, 0)),
        out_shape=jax.ShapeDtypeStruct((bsz, seq, out_width), BF),
        scratch_shapes=scratch,
        compiler_params=_params(2),
        name=name,
    )(x3, *weights)


def _with_history(buf, cur, nb, ts, shifts):
    outs = [[] for _ in shifts]
    for j in range(nb):
        buf[j, CARRY_ROWS:CARRY_ROWS + ts, :] = cur[j * ts:(j + 1) * ts, :]
        for o, d in zip(outs, shifts):
            o.append(buf[j, CARRY_ROWS - d:CARRY_ROWS - d + ts, :])
        buf[j, 0:CARRY_ROWS, :] = buf[j, ts:ts + CARRY_ROWS, :]
    return [jnp.concatenate(o, axis=0) if nb > 1 else o[0] for o in outs]


def _causal_conv(buf, cur, cw_ref, cb_ref, nb, ts):
    out = cb_ref[...] + cw_ref[CONV_WIDTH - 1:CONV_WIDTH, :] * cur
    delayed = _with_history(buf, cur, nb, ts, range(1, CONV_WIDTH))
    for d, prev in zip(range(1, CONV_WIDTH), delayed):
        out = out + cw_ref[CONV_WIDTH - 1 - d:CONV_WIDTH - d, :] * prev
    return out


def _rglru_kernel(x_ref, wx_ref, wg_ref, cw_ref, cb_ref, wa_ref, wi_ref, ba_ref, bi_ref, lam_ref,
                  o_ref, buf, hcar, *, nb, ts):
    @pl.when(pl.program_id(1) == 0)
    def _():
        buf[:, 0:CARRY_ROWS, :] = jnp.zeros((nb, CARRY_ROWS, HALF), F32)
        hcar[...] = jnp.zeros_like(hcar)

    xb = x_ref[...].reshape(nb * ts, D_MODEL).astype(BF)
    xr = _dot(xb, wx_ref[...])
    xg = _dot(xb, wg_ref[...])
    u = _causal_conv(buf, xr, cw_ref, cb_ref, nb, ts)
    ub = u.astype(BF)
    r = jax.nn.sigmoid(_dot(ub, wa_ref[...]) + ba_ref[...])
    i = jax.nn.sigmoid(_dot(ub, wi_ref[...]) + bi_ref[...])
    log_a = -RG_C * r * _softplus(-lam_ref[...])
    a = jnp.exp(log_a)
    bb = jnp.sqrt(1.0 - a * a) * (i * u)
    gate = jax.nn.gelu(xg, approximate=True)
    for j in range(nb):
        rows = slice(j * ts, (j + 1) * ts)
        a_cum, h0 = _linear_scan_rows(a[rows], bb[rows])
        h = h0 + a_cum * hcar[j:j + 1, :]
        hcar[j:j + 1, :] = h[ts - 1:ts, :]
        o_ref[j] = (h * gate[rows]).astype(o_ref.dtype)


def _block_diag(w):
    n, c, d = w.shape
    eye = jnp.eye(n, dtype=w.dtype)
    return (eye[:, None, :, None] * w[:, :, None, :]).reshape(n * c, n * d)


def _rglru(x3, w_x, w_g, conv_w, conv_b, wa, wx, ba, bx, lam, nb, ts):
    weights = [w_x.astype(BF), w_g.astype(BF), conv_w.astype(F32), _row(conv_b),
               _block_diag(wa).astype(BF), _block_diag(wx).astype(BF), _row(ba), _row(bx), _row(lam)]
    scratch = [pltpu.VMEM((nb, ts + CARRY_ROWS, HALF), F32), pltpu.VMEM((max(nb, 8), HALF), F32)]
    return _seq_call(functools.partial(_rglru_kernel, nb=nb, ts=ts), x3, weights, HALF, scratch, nb, ts, "rglru")


def _mlstm_kernel(x_ref, wqk_ref, wv_ref, wo_ref, wif_ref, cw_ref, cb_ref, gb_ref, ng_ref,
                  o_ref, buf, q_s, k_s, v_s, og_s, comb_s, col_s, c_s, m_s, *, nb, ts):
    @pl.when(pl.program_id(1) == 0)
    def _():
        buf[:, 0:CARRY_ROWS, :] = jnp.zeros((nb, CARRY_ROWS, 2 * HALF), F32)
        c_s[...] = jnp.zeros_like(c_s)
        m_s[...] = jnp.zeros_like(m_s)

    xb = x_ref[...].reshape(nb * ts, D_MODEL).astype(BF)
    qk = _causal_conv(buf, _dot(xb, wqk_ref[...]), cw_ref, cb_ref, nb, ts)
    qk = qk * jax.nn.sigmoid(qk)
    q_s[...] = qk[:, :HALF] * (ML_DH ** -0.5)
    k_s[...] = qk[:, HALF:]
    v_s[...] = _dot(xb, wv_ref[...])
    og_s[...] = jax.nn.sigmoid(_dot(xb, wo_ref[...]))
    z = _dot(xb, wif_ref[...]) + gb_ref[...]
    fcum = _split3_left_dot(_chunk_tril(nb * ts), _log_sigmoid(z))
    lane = lax.broadcasted_iota(jnp.int32, z.shape, 1)
    comb = jnp.where(lane < ML_HEADS, z, fcum)
    comb_s[...] = comb
    sel_r = lax.broadcasted_iota(jnp.int32, (LANES, 2 * ML_HEADS * LANES), 0)
    sel_c = lax.broadcasted_iota(jnp.int32, (LANES, 2 * ML_HEADS * LANES), 1)
    sel = jnp.where(jnp.bitwise_and(sel_c, -LANES) == sel_r * LANES, 1.0, 0.0).astype(BF)
    col_s[...] = _split3_right_dot(comb, sel)

    ri = lax.broadcasted_iota(jnp.int32, (CHUNK, CHUNK), 0)
    ci = lax.broadcasted_iota(jnp.int32, (CHUNK, CHUNK), 1)
    causal = ci <= ri
    ones = jnp.ones((CHUNK, LANES), F32)

    units = [(j, h) for j in range(nb) for h in range(ML_HEADS)]
    lanes = [slice(h * ML_DH, (h + 1) * ML_DH) for _, h in units]
    ilanes = [slice(h * LANES, (h + 1) * LANES) for _, h in units]
    flanes = [slice((ML_HEADS + h) * LANES, (ML_HEADS + h + 1) * LANES) for _, h in units]

    def chunk(c, carry):
        off = pl.multiple_of(c * CHUNK, CHUNK)
        seq_rows = [pl.ds(pl.multiple_of(j * ts + off, CHUNK), CHUNK) for j in range(nb)]
        rows = [seq_rows[j] for j, _ in units]
        comb_t = [comb_s[r, :].T for r in seq_rows]
        qb = [q_s[r, l].astype(BF) for r, l in zip(rows, lanes)]
        k = [k_s[r, l] for r, l in zip(rows, lanes)]
        qk = [_dot_nt(a, b.astype(BF)) for a, b in zip(qb, k)]
        vaug = [jnp.concatenate([v_s[r, l], ones], axis=1).astype(BF) for r, l in zip(rows, lanes)]
        caug = [c_s[u] for u in range(len(units))]
        qc = [_dot(a, b.astype(BF)) for a, b in zip(qb, caug)]
        i_col = [col_s[r, l] for r, l in zip(rows, ilanes)]
        f_col = [col_s[r, l] for r, l in zip(rows, flanes)]
        i_row = [comb_t[j][h:h + 1, :] for j, h in units]
        f_row = [comb_t[j][ML_HEADS + h:ML_HEADS + h + 1, :] for j, h in units]
        m = [m_s[u:u + 1, :] for u in range(len(units))]
        d = [jnp.where(causal, fc[:, :CHUNK] - fr + ir, -jnp.inf) for fc, fr, ir in zip(f_col, f_row, i_row)]
        inter = [f + mm for f, mm in zip(f_col, m)]
        m_t = [jnp.maximum(a, jnp.max(b, axis=-1, keepdims=True)) for a, b in zip(inter, d)]
        s = [(a * jnp.exp(b - mt[:, :CHUNK])).astype(BF) for a, b, mt in zip(qk, d, m_t)]
        intra = [_dot(a, b) for a, b in zip(s, vaug)]
        w_inter = [jnp.exp(it - mt) for it, mt in zip(inter, m_t)]
        num = [a[:, :ML_DH] + w * b[:, :ML_DH] for a, w, b in zip(intra, w_inter, qc)]
        den = [a[:, ML_DH:] + w * b[:, ML_DH:] for a, w, b in zip(intra, w_inter, qc)]
        f_last = [f[CHUNK - 1:CHUNK, :] for f in f_col]
        g = [fl - f + i for fl, f, i in zip(f_last, f_col, i_col)]
        m_new = [jnp.maximum(fl + mm, jnp.max(gg, axis=0, keepdims=True)) for fl, mm, gg in zip(f_last, m, g)]
        kw = [(kk * jnp.exp(gg - mn)).T.astype(BF) for kk, gg, mn in zip(k, g, m_new)]
        upd = [_dot(a, b) for a, b in zip(kw, vaug)]
        decay = [jnp.exp(fl + mm - mn) for fl, mm, mn in zip(f_last, m, m_new)]
        hh = [n / jnp.maximum(jnp.abs(dd), jnp.exp(-mt)) for n, dd, mt in zip(num, den, m_t)]
        mu = [jnp.mean(x, axis=-1, keepdims=True) for x in hh]
        hc = [x - y for x, y in zip(hh, mu)]
        var = [jnp.mean(x * x, axis=-1, keepdims=True) for x in hc]
        for u, ((j, _), l) in enumerate(zip(units, lanes)):
            c_s[u] = jnp.concatenate([decay[u], decay[u]], axis=1) * caug[u] + upd[u]
            m_s[u:u + 1, :] = m_new[u]
            hn = hc[u] * lax.rsqrt(var[u] + ML_NORM_EPS) * ng_ref[:, l]
            o_ref[j, pl.ds(off, CHUNK), l] = (hn * og_s[rows[u], l]).astype(o_ref.dtype)
        return carry

    lax.fori_loop(0, ts // CHUNK, chunk, 0)


def _mlstm(x3, w_qk, w_v, w_o, w_if, conv_w, conv_b, i_bias, f_bias, norm_g, nb, ts):
    wif = jnp.pad(w_if, ((0, 0), (0, LANES - 2 * ML_HEADS)))
    gate_bias = jnp.pad(jnp.concatenate([i_bias, f_bias]), (0, LANES - 2 * ML_HEADS))
    weights = [w_qk.astype(BF), w_v.astype(BF), w_o.astype(BF), wif.astype(BF),
               conv_w.astype(F32), _row(conv_b), _row(gate_bias), _row(norm_g)]
    rows = nb * ts
    scratch = [pltpu.VMEM((nb, ts + CARRY_ROWS, 2 * HALF), F32),
               pltpu.VMEM((rows, HALF), F32), pltpu.VMEM((rows, HALF), F32),
               pltpu.VMEM((rows, HALF), F32), pltpu.VMEM((rows, HALF), F32),
               pltpu.VMEM((rows, LANES), F32),
               pltpu.VMEM((rows, 2 * ML_HEADS * LANES), F32),
               pltpu.VMEM((nb * ML_HEADS, ML_DH, 2 * LANES), F32),
               pltpu.VMEM((max(nb * ML_HEADS, 8), LANES), F32)]
    return _seq_call(functools.partial(_mlstm_kernel, nb=nb, ts=ts), x3, weights, HALF, scratch, nb, ts, "mlstm")


def _gla_kernel(x_ref, wq_ref, wk_ref, wv_ref, wgd_ref, wog_ref, gB_ref, gb_ref, ng_ref,
                o_ref, q_s, k_s, v_s, og_s, bc_s, st_s, *, nb, ts):
    @pl.when(pl.program_id(1) == 0)
    def _():
        st_s[...] = jnp.zeros_like(st_s)

    xb = x_ref[...].reshape(nb * ts, D_MODEL).astype(BF)
    q_s[...] = _dot(xb, wq_ref[...]) * (GLA_DK ** -0.5)
    k_s[...] = _dot(xb, wk_ref[...])
    v_s[...] = _dot(xb, wv_ref[...])
    og = _dot(xb, wog_ref[...])
    og_s[...] = og * jax.nn.sigmoid(og)
    gd = _dot(xb, wgd_ref[...])
    log_alpha = _log_sigmoid(_bdot(gd, gB_ref[...]) + gb_ref[...]) / GLA_TAU
    bc_s[...] = _split3_left_dot(_chunk_tril(nb * ts), log_alpha)

    ri = lax.broadcasted_iota(jnp.int32, (CHUNK, CHUNK), 0)
    ci = lax.broadcasted_iota(jnp.int32, (CHUNK, CHUNK), 1)
    causal = ci <= ri

    units = [(j, h) for j in range(nb) for h in range(GLA_HEADS)]
    lanes = [slice(h * LANES, (h + 1) * LANES) for _, h in units]

    def chunk(c, carry):
        off = pl.multiple_of(c * CHUNK, CHUNK)
        rows = [pl.ds(pl.multiple_of(j * ts + off, CHUNK), CHUNK) for j, _ in units]
        bc = [bc_s[r, l] for r, l in zip(rows, lanes)]
        k = [k_s[r, l] for r, l in zip(rows, lanes)]
        v = [v_s[r, l] for r, l in zip(rows, lanes)]
        st = [st_s[u] for u in range(len(units))]
        q_dec = [(q_s[r, l] * jnp.exp(b)).astype(BF) for r, l, b in zip(rows, lanes, bc)]
        qk = [_dot_nt(a, (kk * jnp.exp(-b)).astype(BF)) for a, kk, b in zip(q_dec, k, bc)]
        o_inter = [_dot_nt(a, s.astype(BF)) for a, s in zip(q_dec, st)]
        o = [_dot(jnp.where(causal, a, 0.0).astype(BF), vv.astype(BF)) + b for a, vv, b in zip(qk, v, o_inter)]
        b_last = [b[CHUNK - 1:CHUNK, :] for b in bc]
        k_dec = [(kk * jnp.exp(bl - b)).astype(BF) for kk, bl, b in zip(k, b_last, bc)]
        upd = [_dot(vv.T.astype(BF), kd) for vv, kd in zip(v, k_dec)]
        mu = [jnp.mean(x, axis=-1, keepdims=True) for x in o]
        oc = [x - m for x, m in zip(o, mu)]
        var = [jnp.mean(x * x, axis=-1, keepdims=True) for x in oc]
        for u, ((j, _), l) in enumerate(zip(units, lanes)):
            st_s[u] = st[u] * jnp.exp(b_last[u]) + upd[u]
            on = oc[u] * lax.rsqrt(var[u] + GLA_NORM_EPS) * ng_ref[:, l]
            o_ref[j, pl.ds(off, CHUNK), l] = (on * og_s[rows[u], l]).astype(o_ref.dtype)
        return carry

    lax.fori_loop(0, ts // CHUNK, chunk, 0)


def _gla(x3, w_q, w_k, w_v, w_gd, w_og, gB, gb, norm_g, nb, ts):
    weights = [_pad_heads(w_q, GLA_HEADS, GLA_DK).astype(BF), _pad_heads(w_k, GLA_HEADS, GLA_DK).astype(BF),
               w_v.astype(BF), jnp.pad(w_gd, ((0, 0), (0, LANES - GLA_GATE_LORA))).astype(BF), w_og.astype(BF),
               jnp.pad(_pad_heads(gB, GLA_HEADS, GLA_DK), ((0, LANES - GLA_GATE_LORA), (0, 0))).astype(BF),
               _row(_pad_heads(gb, GLA_HEADS, GLA_DK)), _row(norm_g)]
    wide = GLA_HEADS * LANES
    rows = nb * ts
    scratch = [pltpu.VMEM((rows, wide), F32), pltpu.VMEM((rows, wide), F32),
               pltpu.VMEM((rows, HALF), F32), pltpu.VMEM((rows, HALF), F32),
               pltpu.VMEM((rows, wide), F32),
               pltpu.VMEM((nb * GLA_HEADS, GLA_DV, LANES), F32)]
    return _seq_call(functools.partial(_gla_kernel, nb=nb, ts=ts), x3, weights, HALF, scratch, nb, ts, "gla")


RK_WIDE = RK_HEADS * LANES
RK_PROJ = 3 * RK_WIDE + LANES


def _rwkv_kernel(x_ref, w_ref, mu_ref, w0_ref, wB_ref, a0_ref, aB_ref, gB_ref, kk_ref, ka_ref, rk_ref,
                 lg_ref, lb_ref, o_ref, buf, r_s, k_s, v_s, kap_s, b_s, lw_s, cum_s, g_s, bon_s, h_s, *, nb, ts):
    @pl.when(pl.program_id(1) == 0)
    def _():
        buf[:, 0:CARRY_ROWS, :] = jnp.zeros((nb, CARRY_ROWS, RK_PROJ), F32)
        h_s[...] = jnp.zeros_like(h_s)

    xb = x_ref[...].reshape(nb * ts, D_MODEL).astype(BF)
    p = _dot(xb, w_ref[...])
    prev, = _with_history(buf, p, nb, ts, (1,))
    p = p + mu_ref[...] * (prev - p)
    r = p[:, 0:RK_WIDE]
    k = p[:, RK_WIDE:2 * RK_WIDE]
    v = p[:, 2 * RK_WIDE:3 * RK_WIDE]
    lora = p[:, 3 * RK_WIDE:]
    w_raw = -_softplus(-(w0_ref[...] + _bdot(jnp.tanh(lora), wB_ref[...]))) - 0.5
    lw = -jnp.exp(w_raw)
    a = jax.nn.sigmoid(a0_ref[...] + _bdot(lora, aB_ref[...]))
    g_s[...] = _bdot(jax.nn.sigmoid(lora), gB_ref[...])
    kk = k * kk_ref[...]
    k2 = k * (1.0 + (a - 1.0) * ka_ref[...])
    rkk = r * k2 * rk_ref[...]
    for h in range(RK_HEADS):
        lanes = slice(h * LANES, (h + 1) * LANES)
        kkh = kk[:, lanes]
        nrm = jnp.sqrt(jnp.sum(kkh * kkh, axis=-1, keepdims=True))
        kap = kkh / jnp.maximum(nrm, 1e-12)
        kap_s[:, lanes] = kap
        b_s[:, lanes] = kap * a[:, lanes]
        bon_s[:, lanes] = jnp.sum(rkk[:, lanes], axis=-1, keepdims=True) * v[:, lanes]
    r_s[...] = r
    k_s[...] = k2
    v_s[...] = v
    lw_s[...] = lw
    cum_s[...] = _split3_left_dot(_chunk_tril(nb * ts), lw)

    ri = lax.broadcasted_iota(jnp.int32, (CHUNK, CHUNK), 0)
    ci = lax.broadcasted_iota(jnp.int32, (CHUNK, CHUNK), 1)
    strict = ci < ri
    incl = ci <= ri
    eye_c = jnp.where(ci == ri, 1.0, 0.0)
    rl = lax.broadcasted_iota(jnp.int32, (LANES, LANES), 0)
    cl = lax.broadcasted_iota(jnp.int32, (LANES, LANES), 1)
    eye_l = rl == cl
    real = lax.broadcasted_iota(jnp.int32, (CHUNK, LANES), 1) < RK_DH

    units = [(j, h) for j in range(nb) for h in range(RK_HEADS)]
    lanes = [slice(h * LANES, (h + 1) * LANES) for _, h in units]
    stack = lambda top, bot: jnp.concatenate([top, bot], axis=0)

    def chunk(c, carry):
        off = pl.multiple_of(c * CHUNK, CHUNK)
        rows = [pl.ds(pl.multiple_of(j * ts + off, CHUNK), CHUNK) for j, _ in units]
        cum = [cum_s[r, l] for r, l in zip(rows, lanes)]
        c_last = [x[CHUNK - 1:CHUNK, :] for x in cum]
        ginv = [jnp.exp(-x) for x in cum]
        gend = [jnp.exp(cl - x) for cl, x in zip(c_last, cum)]
        rt = [r_s[r, l] * jnp.exp(x) for r, l, x in zip(rows, lanes, cum)]
        kt = [kap_s[r, l] * jnp.exp(x - lw_s[r, l]) for r, l, x in zip(rows, lanes, cum)]
        k2h = [k_s[r, l] for r, l in zip(rows, lanes)]
        bh = [b_s[r, l] for r, l in zip(rows, lanes)]
        ktb = [x.astype(BF) for x in kt]
        lhs = [stack(a, b.astype(BF)) for a, b in zip(ktb, rt)]
        x1 = [_dot_nt(a, (k * g).astype(BF)) for a, k, g in zip(lhs, k2h, ginv)]
        x2 = [_dot_nt(a, (b * g).astype(BF)) for a, b, g in zip(lhs, bh, ginv)]
        a_kk = [jnp.where(strict, x[:CHUNK], 0.0).astype(BF) for x in x1]
        a_rk = [jnp.where(incl, x[CHUNK:], 0.0) for x in x1]
        a_rb = [jnp.where(incl, x[CHUNK:], 0.0).astype(BF) for x in x2]
        xp = [jnp.where(strict, -x[:CHUNK], 0.0) for x in x2]
        t_inv = [eye_c + x for x in xp]
        xp = [_bdot(x, x) for x in xp]
        for _ in range(4):
            xb = [x.astype(BF) for x in xp]
            z = [_dot(stack(t, x).astype(BF), x) for t, x in zip(t_inv, xb)]
            t_inv = [t + x[:CHUNK] for t, x in zip(t_inv, z)]
            xp = [x[CHUNK:] for x in z]
        t_inv = [(t + _bdot(t, x)).astype(BF) for t, x in zip(t_inv, xp)]
        kp = [_dot(t, k).astype(BF) for t, k in zip(t_inv, ktb)]
        w2 = [_dot(t, a).astype(BF) for t, a in zip(t_inv, a_kk)]
        lhs2 = [stack(a, (b * g).T.astype(BF)) for a, b, g in zip(a_rb, bh, gend)]
        m1 = [_dot(a, k) for a, k in zip(lhs2, kp)]
        m2 = [_dot(a, w) for a, w in zip(lhs2, w2)]
        r_eff = [r - m[:CHUNK] for r, m in zip(rt, m1)]
        g_eff = [jnp.where(eye_l, jnp.exp(cl), 0.0) - m[CHUNK:] for cl, m in zip(c_last, m1)]
        p_eff = [a - m[:CHUNK] for a, m in zip(a_rk, m2)]
        q_eff = [(k * g).T - m[CHUNK:] for k, g, m in zip(k2h, gend, m2)]
        o2 = [_bdot(stack(p, q), v_s[r, l]) for p, q, r, l in zip(p_eff, q_eff, rows, lanes)]
        o1 = [_bdot(stack(r, g), h_s[u]) for u, (r, g) in enumerate(zip(r_eff, g_eff))]
        y = [a[:CHUNK] + b[:CHUNK] for a, b in zip(o1, o2)]
        mu = [jnp.sum(x, axis=-1, keepdims=True) * (1.0 / RK_DH) for x in y]
        yc = [jnp.where(real, x - m, 0.0) for x, m in zip(y, mu)]
        var = [jnp.sum(x * x, axis=-1, keepdims=True) * (1.0 / RK_DH) for x in yc]
        for u, ((j, _), l) in enumerate(zip(units, lanes)):
            h_s[u] = o1[u][CHUNK:] + o2[u][CHUNK:]
            yn = yc[u] * lax.rsqrt(var[u] + RK_NORM_EPS) * lg_ref[:, l] + lb_ref[:, l]
            o_ref[j, pl.ds(off, CHUNK), l] = ((yn + bon_s[rows[u], l]) * g_s[rows[u], l]).astype(o_ref.dtype)
        return carry

    lax.fori_loop(0, ts // CHUNK, chunk, 0)


def _rwkv(x3, w_rk, mu, w0, wB, a0, aB, gB, k_k, k_a, r_k, ln_g, ln_b, nb, ts):
    ph = lambda t: _pad_heads(t, RK_HEADS, RK_DH)
    w_r, w_k, w_v, w_l = (w_rk[:, :HALF], w_rk[:, HALF:2 * HALF], w_rk[:, 2 * HALF:3 * HALF], w_rk[:, 3 * HALF:])
    w_all = jnp.concatenate([ph(w_r), ph(w_k), ph(w_v), w_l], axis=1).astype(BF)
    mu_all = jnp.concatenate([ph(mu[:HALF]), ph(mu[HALF:2 * HALF]), ph(mu[2 * HALF:3 * HALF]), mu[3 * HALF:]])
    d0, d1 = RK_DECAY_LORA, RK_DECAY_LORA + RK_A_LORA
    wB_p = jnp.zeros((LANES, RK_WIDE), F32).at[0:d0].set(ph(wB))
    aB_p = jnp.zeros((LANES, RK_WIDE), F32).at[d0:d1].set(ph(aB))
    gB_p = jnp.zeros((LANES, RK_WIDE), F32).at[d1:].set(ph(gB))
    weights = [w_all, _row(mu_all), _row(ph(w0)), wB_p.astype(BF), _row(ph(a0)), aB_p.astype(BF), gB_p.astype(BF),
               _row(ph(k_k)), _row(ph(k_a)), _row(ph(r_k.reshape(-1))), _row(ph(ln_g)), _row(ph(ln_b))]
    wide = pltpu.VMEM((nb * ts, RK_WIDE), F32)
    scratch = ([pltpu.VMEM((nb, ts + CARRY_ROWS, RK_PROJ), F32)] + [wide] * 9
               + [pltpu.VMEM((nb * RK_HEADS, LANES, LANES), F32)])
    return _seq_call(functools.partial(_rwkv_kernel, nb=nb, ts=ts), x3, weights, RK_WIDE, scratch, nb, ts, "rwkv7")


def _tile(n, pref):
    t = min(n, pref)
    while n % t:
        t //= 2
    return t


def kernel(x, ffn1_wi, ffn1_wo, ffn2_wi, ffn2_wo, ln_g, ln_b, ev_w_in, ev_w_out, rg_conv_w, rg_conv_b, rg_wa, rg_wx, rg_ba, rg_bx, rg_lambda, ml_conv_w, ml_conv_b, ml_i_bias, ml_f_bias, ml_norm_g, od_w_in, od_w_out, rk_mu, rk_w0, rk_wB, rk_a0, rk_aB, rk_gB, rk_k_k, rk_k_a, rk_r_k, rk_ln_g, rk_ln_b, gla_gB, gla_gb, gla_norm_g):
    bsz, seq, d = x.shape
    assert d == D_MODEL and seq % CHUNK == 0
    t = bsz * seq
    tm_ffn = _tile(t, 512)
    tm_out = _tile(t, 256)
    nb =_tile(bsz, SEQS_PER_BLOCK)
    ts = _tile(seq, ROWS_PER_BLOCK // nb)
    rk_in = 3 * HALF + RK_DECAY_LORA + RK_A_LORA + RK_G_LORA
    x2 = x.reshape(t, d)
    for l in range(DEPTH):
        x2 = _ffn_ln(x2, ffn1_wi[l].astype(BF), ffn1_wo[l].astype(BF), ln_g[l, 0], ln_b[l, 0], tm_ffn)
        x3 = x2.reshape(bsz, seq, d)
        if l % 2 == 0:
            e = l // 2
            w = ev_w_in[e]
            ya = _rglru(x3, w[:, 0:HALF], w[:, HALF:2 * HALF], rg_conv_w[e], rg_conv_b[e], rg_wa[e], rg_wx[e],
                        rg_ba[e], rg_bx[e], rg_lambda[e], nb, ts)
            yb = _mlstm(x3, w[:, 2 * HALF:4 * HALF], w[:, 4 * HALF:5 * HALF], w[:, 5 * HALF:6 * HALF],
                        w[:, 6 * HALF:], ml_conv_w[e], ml_conv_b[e], ml_i_bias[e], ml_f_bias[e], ml_norm_g[e], nb, ts)
            wa = ev_w_out[e][:HALF].astype(BF)
            wb = ev_w_out[e][HALF:].astype(BF)
        else:
            o = l // 2
            w = od_w_in[o]
            ya = _rwkv(x3, w[:, :rk_in], rk_mu[o], rk_w0[o], rk_wB[o], rk_a0[o], rk_aB[o], rk_gB[o], rk_k_k[o],
                       rk_k_a[o], rk_r_k[o], rk_ln_g[o], rk_ln_b[o], nb, ts)
            wg = w[:, rk_in:]
            q0, k0, v0, g0 = GLA_HEADS * GLA_DK, 2 * GLA_HEADS * GLA_DK, 2 * GLA_HEADS * GLA_DK + HALF, \
                2 * GLA_HEADS * GLA_DK + HALF + GLA_GATE_LORA
            yb = _gla(x3, wg[:, :q0], wg[:, q0:k0], wg[:, k0:v0], wg[:, v0:g0], wg[:, g0:],
                      gla_gB[o], gla_gb[o], gla_norm_g[o], nb, ts)
            wa = _pad_head_rows(od_w_out[o][:HALF], RK_HEADS, RK_DH).astype(BF)
            wb = od_w_out[o][HALF:].astype(BF)
        x2 = _outproj_ln(ya.reshape(t, -1), yb.reshape(t, -1), x2, wa, wb, ln_g[l, 1], ln_b[l, 1], tm_out)
        x2 = _ffn_ln(x2, ffn2_wi[l].astype(BF), ffn2_wo[l].astype(BF), ln_g[l, 2], ln_b[l, 2], tm_ffn)
    return x2.reshape(bsz, seq, d)
```

```python
import functools
import math

import jax
import jax.numpy as jnp
from jax import lax
from jax.experimental import pallas as pl
from jax.experimental.pallas import tpu as pltpu

F32 = jnp.float32
BF = jnp.bfloat16

D_MODEL = 1024
DEPTH = 4
D_FF = 2816
HALF = 512
CONV_WIDTH = 4
RG_BLOCKS = 8
RG_BLK = HALF // RG_BLOCKS
RG_C = 8.0
ML_HEADS = 4
ML_DH = HALF // ML_HEADS
ML_NORM_EPS = 1e-6
RK_HEADS = 8
RK_DH = HALF // RK_HEADS
RK_DECAY_LORA = 32
RK_A_LORA = 32
RK_G_LORA = 64
RK_NORM_EPS = 64e-5
GLA_HEADS = 4
GLA_DK = 64
GLA_DV = HALF // GLA_HEADS
GLA_GATE_LORA = 16
GLA_TAU = 16.0
GLA_NORM_EPS = 1e-5
DEEPNORM_ALPHA = (2.0 * DEPTH) ** 0.25
LN_EPS = 1e-5

LANES = 128
CHUNK = 64
CARRY_ROWS = 8
SEQS_PER_BLOCK = 2
ROWS_PER_BLOCK = 256
FFN_PARTS = 4
VMEM_LIMIT = 56 * 1024 * 1024


def _dot(a, b):
    return jnp.dot(a, b, preferred_element_type=F32)


def _dot_nt(a, b):
    return lax.dot_general(a, b, (((1,), (1,)), ((), ())), preferred_element_type=F32)


def _bdot(a, b):
    return _dot(a.astype(BF), b.astype(BF))


def _bdot_nt(a, b):
    return _dot_nt(a.astype(BF), b.astype(BF))


def _layer_norm(r, g, b):
    mu = jnp.mean(r, axis=-1, keepdims=True)
    c = r - mu
    var = jnp.mean(c * c, axis=-1, keepdims=True)
    return c * lax.rsqrt(var + LN_EPS) * g + b


def _softplus(z):
    return jnp.maximum(z, 0.0) + jnp.log1p(jnp.exp(-jnp.abs(z)))


def _log_sigmoid(z):
    return -_softplus(-z)


def _split3_left_dot(l_bf, x):
    hi = x.astype(BF)
    r1 = x - hi.astype(F32)
    mid = r1.astype(BF)
    lo = (r1 - mid.astype(F32)).astype(BF)
    return _dot(l_bf, hi) + _dot(l_bf, mid) + _dot(l_bf, lo)


def _split3_right_dot(x, r_bf):
    hi = x.astype(BF)
    r1 = x - hi.astype(F32)
    mid = r1.astype(BF)
    lo = (r1 - mid.astype(F32)).astype(BF)
    return _dot(hi, r_bf) + _dot(mid, r_bf) + _dot(lo, r_bf)


def _chunk_tril(n):
    r = lax.broadcasted_iota(jnp.int32, (n, n), 0)
    c = lax.broadcasted_iota(jnp.int32, (n, n), 1)
    same = jnp.bitwise_and(r, -CHUNK) == jnp.bitwise_and(c, -CHUNK)
    return jnp.where((c <= r) & same, 1.0, 0.0).astype(BF)


SUBLANES = 8


def _linear_scan_rows(a, b, h_prev):
    n, w = a.shape
    groups = n // SUBLANES
    a = a.reshape(groups, SUBLANES, w)
    b = b.reshape(groups, SUBLANES, w)
    sub = lax.broadcasted_iota(jnp.int32, a.shape, 1)
    d = 1
    while d < SUBLANES:
        keep = sub >= d
        a_s = jnp.where(keep, pltpu.roll(a, d, axis=1), 1.0)
        b_s = jnp.where(keep, pltpu.roll(b, d, axis=1), 0.0)
        b = a * b_s + b
        a = a * a_s
        d *= 2
    out = []
    for g in range(groups):
        h = b[g] + a[g] * h_prev
        out.append(h)
        h_prev = h[SUBLANES - 1:SUBLANES, :]
    return jnp.concatenate(out, axis=0), h_prev


def _whole(shape):
    return pl.BlockSpec(shape, lambda *_: (0,) * len(shape), pipeline_mode=pl.Buffered(1))


def _params(n_axes):
    return pltpu.CompilerParams(
        dimension_semantics=("arbitrary",) * n_axes, vmem_limit_bytes=VMEM_LIMIT)


def _pad_heads(w, n_heads, dh):
    lead = w.shape[:-1]
    w = w.reshape(*lead, n_heads, dh)
    w = jnp.pad(w, [(0, 0)] * len(lead) + [(0, 0), (0, LANES - dh)])
    return w.reshape(*lead, n_heads * LANES)


def _row(v):
    return v.reshape(1, -1).astype(F32)


def _ffn_kernel(x_ref, wi_ref, wo_ref, g_ref, b_ref, o_ref, *, parts):
    rows = x_ref.shape[0] // parts
    for i in range(parts):
        sl = slice(i * rows, (i + 1) * rows)
        x = x_ref[sl, :]
        xb = x.astype(BF)
        gate = _dot(xb, wi_ref[:, :D_FF])
        up = _dot(xb, wi_ref[:, D_FF:])
        act = (gate * jax.nn.sigmoid(gate) * up).astype(BF)
        y = _dot(act, wo_ref[...])
        o_ref[sl, :] = _layer_norm(DEEPNORM_ALPHA * x + 0.5 * y, g_ref[...], b_ref[...])


def _ffn_ln(x2, wi, wo, g, b, tm):
    t = x2.shape[0]
    return pl.pallas_call(
        functools.partial(_ffn_kernel, parts=FFN_PARTS),
        grid=(t // tm,),
        in_specs=[pl.BlockSpec((tm, D_MODEL), lambda i: (i, 0)),
                  _whole(wi.shape), _whole(wo.shape), _whole((1, D_MODEL)), _whole((1, D_MODEL))],
        out_specs=pl.BlockSpec((tm, D_MODEL), lambda i: (i, 0)),
        out_shape=jax.ShapeDtypeStruct((t, D_MODEL), F32),
        compiler_params=_params(1),
        name="ffn_ln",
    )(x2, wi, wo, _row(g), _row(b))


def _outproj_kernel(ya_ref, yb_ref, x_ref, wa_ref, wb_ref, g_ref, b_ref, o_ref):
    mix = _dot(ya_ref[...], wa_ref[...]) + _dot(yb_ref[...], wb_ref[...])
    o_ref[...] = _layer_norm(DEEPNORM_ALPHA * x_ref[...] + mix, g_ref[...], b_ref[...])


def _outproj_ln(ya, yb, x2, wa, wb, g, b, tm):
    t = x2.shape[0]
    ka, kb = ya.shape[1], yb.shape[1]
    return pl.pallas_call(
        _outproj_kernel,
        grid=(t // tm,),
        in_specs=[pl.BlockSpec((tm, ka), lambda i: (i, 0)),
                  pl.BlockSpec((tm, kb), lambda i: (i, 0)),
                  pl.BlockSpec((tm, D_MODEL), lambda i: (i, 0)),
                  _whole(wa.shape), _whole(wb.shape), _whole((1, D_MODEL)), _whole((1, D_MODEL))],
        out_specs=pl.BlockSpec((tm, D_MODEL), lambda i: (i, 0)),
        out_shape=jax.ShapeDtypeStruct((t, D_MODEL), F32),
        compiler_params=_params(1),
        name="outproj_ln",
    )(ya, yb, x2, wa, wb, _row(g), _row(b))


def _seq_call(kernel, x3, weights, out_width, scratch, nb, ts, name):
    bsz, seq, _ = x3.shape
    return pl.pallas_call(
        kernel,
        grid=(bsz // nb, seq // ts),
        in_specs=[pl.BlockSpec((nb, ts, D_MODEL), lambda b, s: (b, s, 0))]
                 + [_whole(w.shape) for w in weights],
        out_specs=pl.BlockSpec((nb, ts, out_width), lambda b, s: (b, s, 0)),
        out_shape=jax.ShapeDtypeStruct((bsz, seq, out_width), BF),
        scratch_shapes=scratch,
        compiler_params=_params(2),
        name=name,
    )(x3, *weights)


def _with_history(buf, cur, nb, ts, shifts):
    outs = [[] for _ in shifts]
    for j in range(nb):
        buf[j, CARRY_ROWS:CARRY_ROWS + ts, :] = cur[j * ts:(j + 1) * ts, :]
        for o, d in zip(outs, shifts):
            o.append(buf[j, CARRY_ROWS - d:CARRY_ROWS - d + ts, :])
        buf[j, 0:CARRY_ROWS, :] = buf[j, ts:ts + CARRY_ROWS, :]
    return [jnp.concatenate(o, axis=0) if nb > 1 else o[0] for o in outs]


def _causal_conv(buf, cur, cw_ref, cb_ref, nb, ts):
    out = cb_ref[...] + cw_ref[CONV_WIDTH - 1:CONV_WIDTH, :] * cur
    delayed = _with_history(buf, cur, nb, ts, range(1, CONV_WIDTH))
    for d, prev in zip(range(1, CONV_WIDTH), delayed):
        out = out + cw_ref[CONV_WIDTH - 1 - d:CONV_WIDTH - d, :] * prev
    return out


def _rglru_kernel(x_ref, wx_ref, wg_ref, cw_ref, cb_ref, wa_ref, wi_ref, ba_ref, bi_ref, lam_ref,
                  o_ref, buf, hcar, *, nb, ts):
    @pl.when(pl.program_id(1) == 0)
    def _():
        buf[:, 0:CARRY_ROWS, :] = jnp.zeros((nb, CARRY_ROWS, HALF), F32)
        hcar[...] = jnp.zeros_like(hcar)

    xb = x_ref[...].reshape(nb * ts, D_MODEL).astype(BF)
    xr = _dot(xb, wx_ref[...])
    xg = _dot(xb, wg_ref[...])
    u = _causal_conv(buf, xr, cw_ref, cb_ref, nb, ts)
    ub = u.astype(BF)
    r = jax.nn.sigmoid(_dot(ub, wa_ref[...]) + ba_ref[...])
    i = jax.nn.sigmoid(_dot(ub, wi_ref[...]) + bi_ref[...])
    log_a = -RG_C * r * _softplus(-lam_ref[...])
    a = jnp.exp(log_a)
    bb = jnp.sqrt(1.0 - a * a) * (i * u)
    gate = jax.nn.gelu(xg, approximate=True)
    for j in range(nb):
        rows = slice(j * ts, (j + 1) * ts)
        h, h_last = _linear_scan_rows(a[rows], bb[rows], hcar[j:j + 1, :])
        hcar[j:j + 1, :] = h_last
        o_ref[j] = (h * gate[rows]).astype(o_ref.dtype)


def _block_diag(w):
    n, c, d = w.shape
    eye = jnp.eye(n, dtype=w.dtype)
    return (eye[:, None, :, None] * w[:, :, None, :]).reshape(n * c, n * d)


def _rglru(x3, w_x, w_g, conv_w, conv_b, wa, wx, ba, bx, lam, nb, ts):
    weights = [w_x.astype(BF), w_g.astype(BF), conv_w.astype(F32), _row(conv_b),
               _block_diag(wa).astype(BF), _block_diag(wx).astype(BF), _row(ba), _row(bx), _row(lam)]
    scratch = [pltpu.VMEM((nb, ts + CARRY_ROWS, HALF), F32), pltpu.VMEM((max(nb, 8), HALF), F32)]
    return _seq_call(functools.partial(_rglru_kernel, nb=nb, ts=ts), x3, weights, HALF, scratch, nb, ts, "rglru")


def _mlstm_kernel(x_ref, wqk_ref, wv_ref, wo_ref, wif_ref, cw_ref, cb_ref, gb_ref, ng_ref,
                  o_ref, buf, q_s, k_s, v_s, og_s, comb_s, col_s, c_s, m_s, *, nb, ts):
    @pl.when(pl.program_id(1) == 0)
    def _():
        buf[:, 0:CARRY_ROWS, :] = jnp.zeros((nb, CARRY_ROWS, 2 * HALF), F32)
        c_s[...] = jnp.zeros_like(c_s)
        m_s[...] = jnp.zeros_like(m_s)

    xb = x_ref[...].reshape(nb * ts, D_MODEL).astype(BF)
    qk = _causal_conv(buf, _dot(xb, wqk_ref[...]), cw_ref, cb_ref, nb, ts)
    qk = qk * jax.nn.sigmoid(qk)
    q_s[...] = qk[:, :HALF] * (ML_DH ** -0.5)
    k_s[...] = qk[:, HALF:]
    v_s[...] = _dot(xb, wv_ref[...])
    og_s[...] = jax.nn.sigmoid(_dot(xb, wo_ref[...]))
    z = _dot(xb, wif_ref[...]) + gb_ref[...]
    fcum = _split3_left_dot(_chunk_tril(nb * ts), _log_sigmoid(z))
    lane = lax.broadcasted_iota(jnp.int32, z.shape, 1)
    comb = jnp.where(lane < ML_HEADS, z, fcum)
    comb_s[...] = comb
    sel_r = lax.broadcasted_iota(jnp.int32, (LANES, 2 * ML_HEADS * LANES), 0)
    sel_c = lax.broadcasted_iota(jnp.int32, (LANES, 2 * ML_HEADS * LANES), 1)
    sel = jnp.where(jnp.bitwise_and(sel_c, -LANES) == sel_r * LANES, 1.0, 0.0).astype(BF)
    col_s[...] = _split3_right_dot(comb, sel)

    ri = lax.broadcasted_iota(jnp.int32, (CHUNK, CHUNK), 0)
    ci = lax.broadcasted_iota(jnp.int32, (CHUNK, CHUNK), 1)
    causal = ci <= ri
    ones = jnp.ones((CHUNK, LANES), F32)

    units = [(j, h) for j in range(nb) for h in range(ML_HEADS)]
    lanes = [slice(h * ML_DH, (h + 1) * ML_DH) for _, h in units]
    ilanes = [slice(h * LANES, (h + 1) * LANES) for _, h in units]
    flanes = [slice((ML_HEADS + h) * LANES, (ML_HEADS + h + 1) * LANES) for _, h in units]

    def chunk(c, carry):
        off = pl.multiple_of(c * CHUNK, CHUNK)
        seq_rows = [pl.ds(pl.multiple_of(j * ts + off, CHUNK), CHUNK) for j in range(nb)]
        rows = [seq_rows[j] for j, _ in units]
        comb_t = [comb_s[r, :].T for r in seq_rows]
        qb = [q_s[r, l].astype(BF) for r, l in zip(rows, lanes)]
        k = [k_s[r, l] for r, l in zip(rows, lanes)]
        qk = [_dot_nt(a, b.astype(BF)) for a, b in zip(qb, k)]
        vaug = [jnp.concatenate([v_s[r, l], ones], axis=1).astype(BF) for r, l in zip(rows, lanes)]
        caug = [c_s[u] for u in range(len(units))]
        qc = [_dot(a, b.astype(BF)) for a, b in zip(qb, caug)]
        i_col = [col_s[r, l] for r, l in zip(rows, ilanes)]
        f_col = [col_s[r, l] for r, l in zip(rows, flanes)]
        i_row = [comb_t[j][h:h + 1, :] for j, h in units]
        f_row = [comb_t[j][ML_HEADS + h:ML_HEADS + h + 1, :] for j, h in units]
        m = [m_s[u:u + 1, :] for u in range(len(units))]
        d = [jnp.where(causal, fc[:, :CHUNK] - fr + ir, -jnp.inf) for fc, fr, ir in zip(f_col, f_row, i_row)]
        inter = [f + mm for f, mm in zip(f_col, m)]
        m_t = [jnp.maximum(a, jnp.max(b, axis=-1, keepdims=True)) for a, b in zip(inter, d)]
        s = [(a * jnp.exp(b - mt[:, :CHUNK])).astype(BF) for a, b, mt in zip(qk, d, m_t)]
        intra = [_dot(a, b) for a, b in zip(s, vaug)]
        w_inter = [jnp.exp(it - mt) for it, mt in zip(inter, m_t)]
        num = [a[:, :ML_DH] + w * b[:, :ML_DH] for a, w, b in zip(intra, w_inter, qc)]
        den = [a[:, ML_DH:] + w * b[:, ML_DH:] for a, w, b in zip(intra, w_inter, qc)]
        f_last = [f[CHUNK - 1:CHUNK, :] for f in f_col]
        g = [fl - f + i for fl, f, i in zip(f_last, f_col, i_col)]
        m_new = [jnp.maximum(fl + mm, jnp.max(gg, axis=0, keepdims=True)) for fl, mm, gg in zip(f_last, m, g)]
        kw = [(kk * jnp.exp(gg - mn)).T.astype(BF) for kk, gg, mn in zip(k, g, m_new)]
        upd = [_dot(a, b) for a, b in zip(kw, vaug)]
        decay = [jnp.exp(fl + mm - mn) for fl, mm, mn in zip(f_last, m, m_new)]
        hh = [n / jnp.maximum(jnp.abs(dd), jnp.exp(-mt)) for n, dd, mt in zip(num, den, m_t)]
        mu = [jnp.mean(x, axis=-1, keepdims=True) for x in hh]
        hc = [x - y for x, y in zip(hh, mu)]
        var = [jnp.mean(x * x, axis=-1, keepdims=True) for x in hc]
        for u, ((j, _), l) in enumerate(zip(units, lanes)):
            c_s[u] = jnp.concatenate([decay[u], decay[u]], axis=1) * caug[u] + upd[u]
            m_s[u:u + 1, :] = m_new[u]
            hn = hc[u] * lax.rsqrt(var[u] + ML_NORM_EPS) * ng_ref[:, l]
            o_ref[j, pl.ds(off, CHUNK), l] = (hn * og_s[rows[u], l]).astype(o_ref.dtype)
        return carry

    lax.fori_loop(0, ts // CHUNK, chunk, 0)


def _mlstm(x3, w_qk, w_v, w_o, w_if, conv_w, conv_b, i_bias, f_bias, norm_g, nb, ts):
    wif = jnp.pad(w_if, ((0, 0), (0, LANES - 2 * ML_HEADS)))
    gate_bias = jnp.pad(jnp.concatenate([i_bias, f_bias]), (0, LANES - 2 * ML_HEADS))
    weights = [w_qk.astype(BF), w_v.astype(BF), w_o.astype(BF), wif.astype(BF),
               conv_w.astype(F32), _row(conv_b), _row(gate_bias), _row(norm_g)]
    rows = nb * ts
    scratch = [pltpu.VMEM((nb, ts + CARRY_ROWS, 2 * HALF), F32),
               pltpu.VMEM((rows, HALF), F32), pltpu.VMEM((rows, HALF), F32),
               pltpu.VMEM((rows, HALF), F32), pltpu.VMEM((rows, HALF), F32),
               pltpu.VMEM((rows, LANES), F32),
               pltpu.VMEM((rows, 2 * ML_HEADS * LANES), F32),
               pltpu.VMEM((nb * ML_HEADS, ML_DH, 2 * LANES), F32),
               pltpu.VMEM((max(nb * ML_HEADS, 8), LANES), F32)]
    return _seq_call(functools.partial(_mlstm_kernel, nb=nb, ts=ts), x3, weights, HALF, scratch, nb, ts, "mlstm")


def _gla_kernel(x_ref, wq_ref, wk_ref, wv_ref, wgd_ref, wog_ref, gB_ref, gb_ref, ng_ref,
                o_ref, q_s, k_s, v_s, og_s, bc_s, st_s, *, nb, ts):
    @pl.when(pl.program_id(1) == 0)
    def _():
        st_s[...] = jnp.zeros_like(st_s)

    xb = x_ref[...].reshape(nb * ts, D_MODEL).astype(BF)
    q_s[...] = _dot(xb, wq_ref[...]) * (GLA_DK ** -0.5)
    k_s[...] = _dot(xb, wk_ref[...])
    v_s[...] = _dot(xb, wv_ref[...])
    og = _dot(xb, wog_ref[...])
    og_s[...] = og * jax.nn.sigmoid(og)
    gd = _dot(xb, wgd_ref[...])
    log_alpha = _log_sigmoid(_bdot(gd, gB_ref[...]) + gb_ref[...]) / GLA_TAU
    bc_s[...] = _split3_left_dot(_chunk_tril(nb * ts), log_alpha)

    ri = lax.broadcasted_iota(jnp.int32, (CHUNK, CHUNK), 0)
    ci = lax.broadcasted_iota(jnp.int32, (CHUNK, CHUNK), 1)
    causal = ci <= ri

    n_chunks = ts // CHUNK
    units = [(c, j, h) for c in range(n_chunks) for j in range(nb) for h in range(GLA_HEADS)]
    lanes = [slice(h * LANES, (h + 1) * LANES) for _, _, h in units]
    rows = [slice(j * ts + c * CHUNK, j * ts + (c + 1) * CHUNK) for c, j, _ in units]

    def mix_tile():
        bc = [bc_s[r, l] for r, l in zip(rows, lanes)]
        k = [k_s[r, l] for r, l in zip(rows, lanes)]
        v = [v_s[r, l] for r, l in zip(rows, lanes)]
        q_dec = [(q_s[r, l] * jnp.exp(b)).astype(BF) for r, l, b in zip(rows, lanes, bc)]
        qk = [_dot_nt(a, (kk * jnp.exp(-b)).astype(BF)) for a, kk, b in zip(q_dec, k, bc)]
        o_intra = [_dot(jnp.where(causal, a, 0.0).astype(BF), vv.astype(BF)) for a, vv in zip(qk, v)]
        b_last = [b[CHUNK - 1:CHUNK, :] for b in bc]
        k_dec = [(kk * jnp.exp(bl - b)).astype(BF) for kk, bl, b in zip(k, b_last, bc)]
        upd = [_dot(vv.T.astype(BF), kd) for vv, kd in zip(v, k_dec)]
        decay = [jnp.exp(bl) for bl in b_last]
        per_chunk = nb * GLA_HEADS
        state = [st_s[s] for s in range(per_chunk)]
        o = []
        for u in range(len(units)):
            s = u % per_chunk
            o.append(o_intra[u] + _dot_nt(q_dec[u], state[s].astype(BF)))
            state[s] = state[s] * decay[u] + upd[u]
        for s in range(per_chunk):
            st_s[s] = state[s]
        mu = [jnp.mean(x, axis=-1, keepdims=True) for x in o]
        oc = [x - m for x, m in zip(o, mu)]
        var = [jnp.mean(x * x, axis=-1, keepdims=True) for x in oc]
        for u, ((c, j, _), l) in enumerate(zip(units, lanes)):
            on = oc[u] * lax.rsqrt(var[u] + GLA_NORM_EPS) * ng_ref[:, l]
            o_ref[j, c * CHUNK:(c + 1) * CHUNK, l] = (on * og_s[rows[u], l]).astype(o_ref.dtype)

    mix_tile()


def _gla(x3, w_q, w_k, w_v, w_gd, w_og, gB, gb, norm_g, nb, ts):
    weights = [_pad_heads(w_q, GLA_HEADS, GLA_DK).astype(BF), _pad_heads(w_k, GLA_HEADS, GLA_DK).astype(BF),
               w_v.astype(BF), jnp.pad(w_gd, ((0, 0), (0, LANES - GLA_GATE_LORA))).astype(BF), w_og.astype(BF),
               jnp.pad(_pad_heads(gB, GLA_HEADS, GLA_DK), ((0, LANES - GLA_GATE_LORA), (0, 0))).astype(BF),
               _row(_pad_heads(gb, GLA_HEADS, GLA_DK)), _row(norm_g)]
    wide = GLA_HEADS * LANES
    rows = nb * ts
    scratch = [pltpu.VMEM((rows, wide), F32), pltpu.VMEM((rows, wide), F32),
               pltpu.VMEM((rows, HALF), F32), pltpu.VMEM((rows, HALF), F32),
               pltpu.VMEM((rows, wide), F32),
               pltpu.VMEM((nb * GLA_HEADS, GLA_DV, LANES), F32)]
    return _seq_call(functools.partial(_gla_kernel, nb=nb, ts=ts), x3, weights, HALF, scratch, nb, ts, "gla")


RK_PAIRS = RK_HEADS // 2
RK_IN_W = 3 * HALF + LANES


def _rwkv_pair_kernel(x_ref, w_ref, mu_ref, w0_ref, wB_ref, a0_ref, aB_ref, gB_ref, kk_ref, ka_ref, rk_ref,
                      lg_ref, lb_ref, o_ref, buf, r_s, k_s, v_s, kap_s, b_s, lw_s, cum_s, g_s, bon_s, h_s, *, nb, ts):
    @pl.when(pl.program_id(1) == 0)
    def _():
        buf[:, 0:CARRY_ROWS, :] = jnp.zeros((nb, CARRY_ROWS, RK_IN_W), F32)
        h_s[...] = jnp.zeros_like(h_s)

    rl = lax.broadcasted_iota(jnp.int32, (LANES, LANES), 0)
    cl = lax.broadcasted_iota(jnp.int32, (LANES, LANES), 1)
    same_head = (rl < RK_DH) == (cl < RK_DH)
    ones_bd = jnp.where(same_head, 1.0, 0.0).astype(BF)
    eye_l = rl == cl

    xb = x_ref[...].reshape(nb * ts, D_MODEL).astype(BF)
    p = _dot(xb, w_ref[...])
    prev, = _with_history(buf, p, nb, ts, (1,))
    p = p + mu_ref[...] * (prev - p)
    r = p[:, 0:HALF]
    k = p[:, HALF:2 * HALF]
    v = p[:, 2 * HALF:3 * HALF]
    lora = p[:, 3 * HALF:]
    w_raw = -_softplus(-(w0_ref[...] + _bdot(jnp.tanh(lora), wB_ref[...]))) - 0.5
    lw = -jnp.exp(w_raw)
    a = jax.nn.sigmoid(a0_ref[...] + _bdot(lora, aB_ref[...]))
    g_s[...] = _bdot(jax.nn.sigmoid(lora), gB_ref[...])
    kk = k * kk_ref[...]
    k2 = k * (1.0 + (a - 1.0) * ka_ref[...])
    rkk = r * k2 * rk_ref[...]
    for q in range(RK_PAIRS):
        lanes = slice(q * LANES, (q + 1) * LANES)
        kkq = kk[:, lanes]
        nrm = jnp.sqrt(_split3_right_dot(kkq * kkq, ones_bd))
        kap = kkq / jnp.maximum(nrm, 1e-12)
        kap_s[:, lanes] = kap
        b_s[:, lanes] = kap * a[:, lanes]
        bon_s[:, lanes] = _split3_right_dot(rkk[:, lanes], ones_bd) * v[:, lanes]
    r_s[...] = r
    k_s[...] = k2
    v_s[...] = v
    lw_s[...] = lw
    cum_s[...] = _split3_left_dot(_chunk_tril(nb * ts), lw)

    ri = lax.broadcasted_iota(jnp.int32, (CHUNK, LANES), 0)
    ci = jnp.bitwise_and(lax.broadcasted_iota(jnp.int32, (CHUNK, LANES), 1), CHUNK - 1)
    strict = ci < ri
    incl = ci <= ri
    eye2 = jnp.where(ci == ri, 1.0, 0.0)
    head_a = lax.broadcasted_iota(jnp.int32, (CHUNK, LANES), 1) < RK_DH

    n_chunks = ts // CHUNK
    units = [(c, j, q) for c in range(n_chunks) for j in range(nb) for q in range(RK_PAIRS)]
    lanes = [slice(q * LANES, (q + 1) * LANES) for _, _, q in units]
    rows = [slice(j * ts + c * CHUNK, j * ts + (c + 1) * CHUNK) for c, j, _ in units]
    stack = lambda top, bot: jnp.concatenate([top, bot], axis=0)

    def bd(x):
        return stack(jnp.where(head_a, x, 0.0), jnp.where(head_a, 0.0, x)).astype(BF)

    def bd_t(x):
        xt = x.T
        return jnp.where(same_head, jnp.concatenate([xt, xt], axis=1), 0.0)

    def mix_tile():
        cum = [cum_s[r_, l] for r_, l in zip(rows, lanes)]
        c_last = [x[CHUNK - 1:CHUNK, :] for x in cum]
        ginv = [jnp.exp(-x) for x in cum]
        gend = [jnp.exp(cl_ - x) for cl_, x in zip(c_last, cum)]
        rt = [r_s[r_, l] * jnp.exp(x) for r_, l, x in zip(rows, lanes, cum)]
        kt = [kap_s[r_, l] * jnp.exp(x - lw_s[r_, l]) for r_, l, x in zip(rows, lanes, cum)]
        k2h = [k_s[r_, l] for r_, l in zip(rows, lanes)]
        bh = [b_s[r_, l] for r_, l in zip(rows, lanes)]
        lhs = [stack(a_, b_).astype(BF) for a_, b_ in zip(kt, rt)]
        x1 = [_dot_nt(a_, bd(k_ * g_)) for a_, k_, g_ in zip(lhs, k2h, ginv)]
        x2 = [_dot_nt(a_, bd(b_ * g_)) for a_, b_, g_ in zip(lhs, bh, ginv)]
        a_kk = [jnp.where(strict, x[:CHUNK], 0.0) for x in x1]
        a_rk = [jnp.where(incl, x[CHUNK:], 0.0) for x in x1]
        a_rb = [jnp.where(incl, x[CHUNK:], 0.0).astype(BF) for x in x2]
        xp = [jnp.where(strict, -x[:CHUNK], 0.0) for x in x2]
        t_inv = [eye2 + x for x in xp]
        xp = [_dot(x.astype(BF), bd(x)) for x in xp]
        for _ in range(4):
            z = [_dot(stack(t, x).astype(BF), bd(x)) for t, x in zip(t_inv, xp)]
            t_inv = [t + x[:CHUNK] for t, x in zip(t_inv, z)]
            xp = [x[CHUNK:] for x in z]
        t_inv = [(t + _dot(t.astype(BF), bd(x))).astype(BF) for t, x in zip(t_inv, xp)]
        kp = [bd(_dot(t, bd(k_))) for t, k_ in zip(t_inv, kt)]
        w2 = [bd(_dot(t, bd(a_))) for t, a_ in zip(t_inv, a_kk)]
        lhs2 = [stack(a_, bd_t(b_ * g_).astype(BF)) for a_, b_, g_ in zip(a_rb, bh, gend)]
        m1 = [_dot(a_, k_) for a_, k_ in zip(lhs2, kp)]
        m2 = [_dot(a_, w_) for a_, w_ in zip(lhs2, w2)]
        r_eff = [r_ - m[:CHUNK] for r_, m in zip(rt, m1)]
        g_eff = [jnp.where(eye_l, jnp.exp(cl_), 0.0) - m[CHUNK:] for cl_, m in zip(c_last, m1)]
        p_eff = [a_ - m[:CHUNK] for a_, m in zip(a_rk, m2)]
        q_eff = [bd_t(k_ * g_) - m[CHUNK:] for k_, g_, m in zip(k2h, gend, m2)]
        o2 = [_dot(stack(p_, q_).astype(BF), bd(v_s[r_, l])) for p_, q_, r_, l in zip(p_eff, q_eff, rows, lanes)]
        lhs3 = [stack(r_, g_).astype(BF) for r_, g_ in zip(r_eff, g_eff)]
        per_chunk = nb * RK_PAIRS
        state = [h_s[s] for s in range(per_chunk)]
        y = []
        for u in range(len(units)):
            s = u % per_chunk
            o1 = _dot(lhs3[u], state[s].astype(BF))
            y.append(o1[:CHUNK] + o2[u][:CHUNK])
            state[s] = o1[CHUNK:] + o2[u][CHUNK:]
        for s in range(per_chunk):
            h_s[s] = state[s]
        seg = lambda x: jnp.where(head_a, jnp.sum(jnp.where(head_a, x, 0.0), axis=-1, keepdims=True),
                                  jnp.sum(jnp.where(head_a, 0.0, x), axis=-1, keepdims=True))
        mu = [seg(x) * (1.0 / RK_DH) for x in y]
        yc = [x - m for x, m in zip(y, mu)]
        var = [seg(x * x) * (1.0 / RK_DH) for x in yc]
        for u, ((c, j, _), l) in enumerate(zip(units, lanes)):
            yn = yc[u] * lax.rsqrt(var[u] + RK_NORM_EPS) * lg_ref[:, l] + lb_ref[:, l]
            o_ref[j, c * CHUNK:(c + 1) * CHUNK, l] = ((yn + bon_s[rows[u], l]) * g_s[rows[u], l]).astype(o_ref.dtype)

    mix_tile()


def _rwkv_pair(x3, w_rk, mu, w0, wB, a0, aB, gB, k_k, k_a, r_k, ln_g, ln_b, nb, ts):
    d0, d1 = RK_DECAY_LORA, RK_DECAY_LORA + RK_A_LORA
    wB_p = jnp.zeros((LANES, HALF), F32).at[0:d0].set(wB)
    aB_p = jnp.zeros((LANES, HALF), F32).at[d0:d1].set(aB)
    gB_p = jnp.zeros((LANES, HALF), F32).at[d1:].set(gB)
    weights = [w_rk.astype(BF), _row(mu), _row(w0), wB_p.astype(BF), _row(a0), aB_p.astype(BF), gB_p.astype(BF),
               _row(k_k), _row(k_a), _row(r_k.reshape(-1)), _row(ln_g), _row(ln_b)]
    wide = pltpu.VMEM((nb * ts, HALF), F32)
    scratch = ([pltpu.VMEM((nb, ts + CARRY_ROWS, RK_IN_W), F32)] + [wide] * 9
               + [pltpu.VMEM((nb * RK_PAIRS, LANES, LANES), F32)])
    return _seq_call(functools.partial(_rwkv_pair_kernel, nb=nb, ts=ts), x3, weights, HALF, scratch, nb, ts,
                     "rwkv7")


def _tile(n, pref):
    t = min(n, pref)
    while n % t:
        t //= 2
    return t


def kernel(x, ffn1_wi, ffn1_wo, ffn2_wi, ffn2_wo, ln_g, ln_b, ev_w_in, ev_w_out, rg_conv_w, rg_conv_b, rg_wa, rg_wx, rg_ba, rg_bx, rg_lambda, ml_conv_w, ml_conv_b, ml_i_bias, ml_f_bias, ml_norm_g, od_w_in, od_w_out, rk_mu, rk_w0, rk_wB, rk_a0, rk_aB, rk_gB, rk_k_k, rk_k_a, rk_r_k, rk_ln_g, rk_ln_b, gla_gB, gla_gb, gla_norm_g):
    bsz, seq, d = x.shape
    assert d == D_MODEL and seq % CHUNK == 0
    t = bsz * seq
    tm_ffn = _tile(t, 1024)
    tm_out = _tile(t, 256)
    nb =_tile(bsz, SEQS_PER_BLOCK)
    ts = _tile(seq, ROWS_PER_BLOCK // nb)
    rk_in = 3 * HALF + RK_DECAY_LORA + RK_A_LORA + RK_G_LORA
    x2 = x.reshape(t, d)
    for l in range(DEPTH):
        x2 = _ffn_ln(x2, ffn1_wi[l].astype(BF), ffn1_wo[l].astype(BF), ln_g[l, 0], ln_b[l, 0], tm_ffn)
        x3 = x2.reshape(bsz, seq, d)
        if l % 2 == 0:
            e = l // 2
            w = ev_w_in[e]
            ya = _rglru(x3, w[:, 0:HALF], w[:, HALF:2 * HALF], rg_conv_w[e], rg_conv_b[e], rg_wa[e], rg_wx[e],
                        rg_ba[e], rg_bx[e], rg_lambda[e], nb, ts)
            yb = _mlstm(x3, w[:, 2 * HALF:4 * HALF], w[:, 4 * HALF:5 * HALF], w[:, 5 * HALF:6 * HALF],
                        w[:, 6 * HALF:], ml_conv_w[e], ml_conv_b[e], ml_i_bias[e], ml_f_bias[e], ml_norm_g[e], nb, ts)
            wa = ev_w_out[e][:HALF].astype(BF)
            wb = ev_w_out[e][HALF:].astype(BF)
        else:
            o = l // 2
            w = od_w_in[o]
            ya = _rwkv_pair(x3, w[:, :rk_in], rk_mu[o], rk_w0[o], rk_wB[o], rk_a0[o], rk_aB[o], rk_gB[o], rk_k_k[o],
                       rk_k_a[o], rk_r_k[o], rk_ln_g[o], rk_ln_b[o], nb, ts)
            wg = w[:, rk_in:]
            q0, k0, v0, g0 = GLA_HEADS * GLA_DK, 2 * GLA_HEADS * GLA_DK, 2 * GLA_HEADS * GLA_DK + HALF, \
                2 * GLA_HEADS * GLA_DK + HALF + GLA_GATE_LORA
            yb = _gla(x3, wg[:, :q0], wg[:, q0:k0], wg[:, k0:v0], wg[:, v0:g0], wg[:, g0:],
                      gla_gB[o], gla_gb[o], gla_norm_g[o], nb, ts)
            wa = od_w_out[o][:HALF].astype(BF)
            wb = od_w_out[o][HALF:].astype(BF)
        x2 = _outproj_ln(ya.reshape(t, -1), yb.reshape(t, -1), x2, wa, wb, ln_g[l, 1], ln_b[l, 1], tm_out)
        x2 = _ffn_ln(x2, ffn2_wi[l].astype(BF), ffn2_wo[l].astype(BF), ln_g[l, 2], ln_b[l, 2], tm_ffn)
    return x2.reshape(bsz, seq, d)
```

```python
import functools

import jax
import jax.numpy as jnp
from jax import lax
from jax.experimental import pallas as pl
from jax.experimental.pallas import tpu as pltpu

F32 = jnp.float32
BF = jnp.bfloat16

D_MODEL = 1024
DEPTH = 4
D_FF = 2816
HALF = 512
CONV_WIDTH = 4
RG_BLOCKS = 8
RG_C = 8.0
ML_HEADS = 4
ML_DH = HALF // ML_HEADS
ML_NORM_EPS = 1e-6
RK_HEADS = 8
RK_DH = HALF // RK_HEADS
RK_DECAY_LORA = 32
RK_A_LORA = 32
RK_G_LORA = 64
RK_NORM_EPS = 64e-5
GLA_HEADS = 4
GLA_DK = 64
GLA_DV = HALF // GLA_HEADS
GLA_GATE_LORA = 16
GLA_TAU = 16.0
GLA_NORM_EPS = 1e-5
DEEPNORM_ALPHA = (2.0 * DEPTH) ** 0.25
LN_EPS = 1e-5

LANES = 128
SUBLANES = 8
CHUNK = 64
CARRY_ROWS = 8
VMEM_LIMIT = 56 * 1024 * 1024
SEQS_PER_BLOCK = 2
ROWS_PER_BLOCK = 256
FFN_ROWS = 1024
FFN_PARTS = 4


def _dot(a, b):
    return jnp.dot(a, b, preferred_element_type=F32)


def _dot_nt(a, b):
    return lax.dot_general(a, b, (((1,), (1,)), ((), ())), preferred_element_type=F32)


def _bdot(a, b):
    return _dot(a.astype(BF), b.astype(BF))


def _layer_norm(r, g, b):
    mu = jnp.mean(r, axis=-1, keepdims=True)
    c = r - mu
    var = jnp.mean(c * c, axis=-1, keepdims=True)
    return c * lax.rsqrt(var + LN_EPS) * g + b


def _softplus(z):
    return jnp.maximum(z, 0.0) + jnp.log1p(jnp.exp(-jnp.abs(z)))


def _log_sigmoid(z):
    return -_softplus(-z)


def _split_hi_lo(x):
    hi = x.astype(BF)
    lo = (x - hi.astype(F32)).astype(BF)
    return hi, lo


def _split_left_dot(l_bf, x):
    hi, lo = _split_hi_lo(x)
    return _dot(l_bf, hi) + _dot(l_bf, lo)


def _split_right_dot(x, r_bf):
    hi, lo = _split_hi_lo(x)
    return _dot(hi, r_bf) + _dot(lo, r_bf)


def _chunk_tril(n):
    r = lax.broadcasted_iota(jnp.int32, (n, n), 0)
    c = lax.broadcasted_iota(jnp.int32, (n, n), 1)
    same = jnp.bitwise_and(r, -CHUNK) == jnp.bitwise_and(c, -CHUNK)
    return jnp.where((c <= r) & same, 1.0, 0.0).astype(BF)


def _linear_scan_rows(a, b, h_prev):
    n, w = a.shape
    groups = n // SUBLANES
    a = a.reshape(groups, SUBLANES, w)
    b = b.reshape(groups, SUBLANES, w)
    sub = lax.broadcasted_iota(jnp.int32, a.shape, 1)
    d = 1
    while d < SUBLANES:
        keep = sub >= d
        a_s = jnp.where(keep, pltpu.roll(a, d, axis=1), 1.0)
        b_s = jnp.where(keep, pltpu.roll(b, d, axis=1), 0.0)
        b = a * b_s + b
        a = a * a_s
        d *= 2
    out = []
    for g in range(groups):
        h = b[g] + a[g] * h_prev
        out.append(h)
        h_prev = h[SUBLANES - 1:SUBLANES, :]
    return jnp.concatenate(out, axis=0), h_prev


def _whole(shape):
    return pl.BlockSpec(shape, lambda *_: (0,) * len(shape), pipeline_mode=pl.Buffered(1))


def _params(n_axes):
    return pltpu.CompilerParams(
        dimension_semantics=("arbitrary",) * n_axes, vmem_limit_bytes=VMEM_LIMIT)


def _pad_heads(w, n_heads, dh):
    lead = w.shape[:-1]
    w = w.reshape(*lead, n_heads, dh)
    w = jnp.pad(w, [(0, 0)] * len(lead) + [(0, 0), (0, LANES - dh)])
    return w.reshape(*lead, n_heads * LANES)


def _row(v):
    return v.reshape(1, -1).astype(F32)


def _stack(top, bot):
    return jnp.concatenate([top, bot], axis=0)


def _ffn_kernel(x_ref, wi_ref, wo_ref, g_ref, b_ref, o_ref, *, parts):
    rows = x_ref.shape[0] // parts
    for i in range(parts):
        sl = slice(i * rows, (i + 1) * rows)
        x = x_ref[sl, :]
        xb = x.astype(BF)
        gate = _dot(xb, wi_ref[:, :D_FF])
        up = _dot(xb, wi_ref[:, D_FF:])
        act = (gate * jax.nn.sigmoid(gate) * up).astype(BF)
        y = _dot(act, wo_ref[...])
        o_ref[sl, :] = _layer_norm(DEEPNORM_ALPHA * x + 0.5 * y, g_ref[...], b_ref[...])


def _ffn_ln(x2, wi, wo, g, b, tm):
    t = x2.shape[0]
    return pl.pallas_call(
        functools.partial(_ffn_kernel, parts=FFN_PARTS),
        grid=(t // tm,),
        in_specs=[pl.BlockSpec((tm, D_MODEL), lambda i: (i, 0)),
                  _whole(wi.shape), _whole(wo.shape), _whole((1, D_MODEL)), _whole((1, D_MODEL))],
        out_specs=pl.BlockSpec((tm, D_MODEL), lambda i: (i, 0)),
        out_shape=jax.ShapeDtypeStruct((t, D_MODEL), F32),
        compiler_params=_params(1),
        name="ffn_ln",
    )(x2, wi, wo, _row(g), _row(b))


def _with_history(buf, cur, nb, ts, shifts):
    outs = [[] for _ in shifts]
    for j in range(nb):
        buf[j, CARRY_ROWS:CARRY_ROWS + ts, :] = cur[j * ts:(j + 1) * ts, :]
        for o, d in zip(outs, shifts):
            o.append(buf[j, CARRY_ROWS - d:CARRY_ROWS - d + ts, :])
        buf[j, 0:CARRY_ROWS, :] = buf[j, ts:ts + CARRY_ROWS, :]
    return [jnp.concatenate(o, axis=0) if nb > 1 else o[0] for o in outs]


def _causal_conv(buf, cur, cw_ref, cb_ref, nb, ts):
    out = cb_ref[...] + cw_ref[CONV_WIDTH - 1:CONV_WIDTH, :] * cur
    delayed = _with_history(buf, cur, nb, ts, range(1, CONV_WIDTH))
    for d, prev in zip(range(1, CONV_WIDTH), delayed):
        out = out + cw_ref[CONV_WIDTH - 1 - d:CONV_WIDTH - d, :] * prev
    return out


def _zero_history(buf):
    nb, _, w = buf.shape
    buf[:, 0:CARRY_ROWS, :] = jnp.zeros((nb, CARRY_ROWS, w), F32)


def _rglru_init(buf, hcar):
    _zero_history(buf)
    hcar[...] = jnp.zeros_like(hcar)


def _rglru_body(xb, wx_ref, wg_ref, cw_ref, cb_ref, wa_ref, wi_ref, ba_ref, bi_ref, lam_ref,
                y_ref, buf, hcar, *, nb, ts):
    xr = _dot(xb, wx_ref[...])
    xg = _dot(xb, wg_ref[...])
    u = _causal_conv(buf, xr, cw_ref, cb_ref, nb, ts)
    ub = u.astype(BF)
    r = jax.nn.sigmoid(_dot(ub, wa_ref[...]) + ba_ref[...])
    i = jax.nn.sigmoid(_dot(ub, wi_ref[...]) + bi_ref[...])
    log_a = -RG_C * r * _softplus(-lam_ref[...])
    a = jnp.exp(log_a)
    bb = jnp.sqrt(1.0 - a * a) * (i * u)
    gate = jax.nn.gelu(xg, approximate=True)
    for j in range(nb):
        rows = slice(j * ts, (j + 1) * ts)
        h, h_last = _linear_scan_rows(a[rows], bb[rows], hcar[j:j + 1, :])
        hcar[j:j + 1, :] = h_last
        y_ref[j] = (h * gate[rows]).astype(y_ref.dtype)


def _block_diag(w):
    n, c, d = w.shape
    eye = jnp.eye(n, dtype=w.dtype)
    return (eye[:, None, :, None] * w[:, :, None, :]).reshape(n * c, n * d)


def _rglru_setup(w_x, w_g, conv_w, conv_b, wa, wx, ba, bx, lam, nb, ts):
    weights = [w_x.astype(BF), w_g.astype(BF), conv_w.astype(F32), _row(conv_b),
               _block_diag(wa).astype(BF), _block_diag(wx).astype(BF), _row(ba), _row(bx), _row(lam)]
    scratch = [pltpu.VMEM((nb, ts + CARRY_ROWS, HALF), F32), pltpu.VMEM((max(nb, SUBLANES), HALF), F32)]
    return weights, scratch


def _mlstm_init(buf, c_s, m_s):
    _zero_history(buf)
    c_s[...] = jnp.zeros_like(c_s)
    m_s[...] = jnp.zeros_like(m_s)


def _mlstm_body(xb, wqk_ref, wv_ref, wo_ref, wif_ref, cw_ref, cb_ref, gb_ref, ng_ref,
                y_ref, buf, c_s, m_s, *, nb, ts):
    qk = _causal_conv(buf, _dot(xb, wqk_ref[...]), cw_ref, cb_ref, nb, ts)
    qk = qk * jax.nn.sigmoid(qk)
    q_all = qk[:, :HALF] * (ML_DH ** -0.5)
    k_all = qk[:, HALF:]
    v_all = _dot(xb, wv_ref[...])
    og_all = jax.nn.sigmoid(_dot(xb, wo_ref[...]))
    z = _dot(xb, wif_ref[...]) + gb_ref[...]
    fcum = _split_left_dot(_chunk_tril(nb * ts), _log_sigmoid(z))
    lane = lax.broadcasted_iota(jnp.int32, z.shape, 1)
    comb = jnp.where(lane < ML_HEADS, z, fcum)
    sel_r = lax.broadcasted_iota(jnp.int32, (LANES, 2 * ML_HEADS * LANES), 0)
    sel_c = lax.broadcasted_iota(jnp.int32, (LANES, 2 * ML_HEADS * LANES), 1)
    sel = jnp.where(jnp.bitwise_and(sel_c, -LANES) == sel_r * LANES, 1.0, 0.0).astype(BF)
    col = _split_right_dot(comb, sel)

    ri = lax.broadcasted_iota(jnp.int32, (CHUNK, CHUNK), 0)
    ci = lax.broadcasted_iota(jnp.int32, (CHUNK, CHUNK), 1)
    causal = ci <= ri
    ones = jnp.ones((CHUNK, LANES), F32)

    chains = [(j, h) for j in range(nb) for h in range(ML_HEADS)]
    state = [c_s[u] for u in range(len(chains))]
    m = [m_s[u:u + 1, :] for u in range(len(chains))]
    for c in range(ts // CHUNK):
        rows = [slice(j * ts + c * CHUNK, j * ts + (c + 1) * CHUNK) for j, _ in chains]
        lanes = [slice(h * ML_DH, (h + 1) * ML_DH) for _, h in chains]
        comb_t = [comb[j * ts + c * CHUNK:j * ts + (c + 1) * CHUNK, :].T for j in range(nb)]
        qb = [q_all[r, l].astype(BF) for r, l in zip(rows, lanes)]
        k = [k_all[r, l] for r, l in zip(rows, lanes)]
        qk_c = [_dot_nt(a, b.astype(BF)) for a, b in zip(qb, k)]
        vaug = [jnp.concatenate([v_all[r, l], ones], axis=1).astype(BF) for r, l in zip(rows, lanes)]
        qc = [_dot(a, b.astype(BF)) for a, b in zip(qb, state)]
        i_col = [col[r, h * LANES:(h + 1) * LANES] for r, (_, h) in zip(rows, chains)]
        f_col = [col[r, (ML_HEADS + h) * LANES:(ML_HEADS + h + 1) * LANES] for r, (_, h) in zip(rows, chains)]
        i_row = [comb_t[j][h:h + 1, :] for j, h in chains]
        f_row = [comb_t[j][ML_HEADS + h:ML_HEADS + h + 1, :] for j, h in chains]
        d = [jnp.where(causal, fc[:, :CHUNK] - fr + ir, -jnp.inf) for fc, fr, ir in zip(f_col, f_row, i_row)]
        inter = [f + mm for f, mm in zip(f_col, m)]
        m_t = [jnp.maximum(a, jnp.max(b, axis=-1, keepdims=True)) for a, b in zip(inter, d)]
        s = [(a * jnp.exp(b - mt[:, :CHUNK])).astype(BF) for a, b, mt in zip(qk_c, d, m_t)]
        intra = [_dot(a, b) for a, b in zip(s, vaug)]
        w_inter = [jnp.exp(it - mt) for it, mt in zip(inter, m_t)]
        num = [a[:, :ML_DH] + w * b[:, :ML_DH] for a, w, b in zip(intra, w_inter, qc)]
        den = [a[:, ML_DH:] + w * b[:, ML_DH:] for a, w, b in zip(intra, w_inter, qc)]
        f_last = [f[CHUNK - 1:CHUNK, :] for f in f_col]
        g = [fl - f + i for fl, f, i in zip(f_last, f_col, i_col)]
        m_new = [jnp.maximum(fl + mm, jnp.max(gg, axis=0, keepdims=True)) for fl, mm, gg in zip(f_last, m, g)]
        kw = [(kk * jnp.exp(gg - mn)).T.astype(BF) for kk, gg, mn in zip(k, g, m_new)]
        upd = [_dot(a, b) for a, b in zip(kw, vaug)]
        decay = [jnp.exp(fl + mm - mn) for fl, mm, mn in zip(f_last, m, m_new)]
        state = [jnp.concatenate([dc, dc], axis=1) * st + up for dc, st, up in zip(decay, state, upd)]
        m = m_new
        hh = [n / jnp.maximum(jnp.abs(dd), jnp.exp(-mt)) for n, dd, mt in zip(num, den, m_t)]
        mu = [jnp.mean(x, axis=-1, keepdims=True) for x in hh]
        hc = [x - y for x, y in zip(hh, mu)]
        var = [jnp.mean(x * x, axis=-1, keepdims=True) for x in hc]
        for u, ((j, _), l) in enumerate(zip(chains, lanes)):
            hn = hc[u] * lax.rsqrt(var[u] + ML_NORM_EPS) * ng_ref[:, l]
            y_ref[j, c * CHUNK:(c + 1) * CHUNK, l] = (hn * og_all[rows[u], l]).astype(y_ref.dtype)
    for u in range(len(chains)):
        c_s[u] = state[u]
        m_s[u:u + 1, :] = m[u]


def _mlstm_setup(w_qk, w_v, w_o, w_if, conv_w, conv_b, i_bias, f_bias, norm_g, nb, ts):
    wif = jnp.pad(w_if, ((0, 0), (0, LANES - 2 * ML_HEADS)))
    gate_bias = jnp.pad(jnp.concatenate([i_bias, f_bias]), (0, LANES - 2 * ML_HEADS))
    weights = [w_qk.astype(BF), w_v.astype(BF), w_o.astype(BF), wif.astype(BF),
               conv_w.astype(F32), _row(conv_b), _row(gate_bias), _row(norm_g)]
    scratch = [pltpu.VMEM((nb, ts + CARRY_ROWS, 2 * HALF), F32),
               pltpu.VMEM((nb * ML_HEADS, ML_DH, 2 * LANES), F32),
               pltpu.VMEM((max(nb * ML_HEADS, SUBLANES), LANES), F32)]
    return weights, scratch


def _gla_init(st_s):
    st_s[...] = jnp.zeros_like(st_s)


def _gla_body(xb, wq_ref, wk_ref, wv_ref, wgd_ref, wog_ref, gB_ref, gb_ref, ng_ref,
              y_ref, st_s, *, nb, ts):
    q_all = _dot(xb, wq_ref[...]) * (GLA_DK ** -0.5)
    k_all = _dot(xb, wk_ref[...])
    v_all = _dot(xb, wv_ref[...])
    og = _dot(xb, wog_ref[...])
    og_all = og * jax.nn.sigmoid(og)
    gd = _dot(xb, wgd_ref[...])
    log_alpha = _log_sigmoid(_bdot(gd, gB_ref[...]) + gb_ref[...]) / GLA_TAU
    bc_all = _split_left_dot(_chunk_tril(nb * ts), log_alpha)

    ri = lax.broadcasted_iota(jnp.int32, (CHUNK, CHUNK), 0)
    ci = lax.broadcasted_iota(jnp.int32, (CHUNK, CHUNK), 1)
    causal = ci <= ri

    units = [(c, j, h) for c in range(ts // CHUNK) for j in range(nb) for h in range(GLA_HEADS)]
    lanes = [slice(h * LANES, (h + 1) * LANES) for _, _, h in units]
    rows = [slice(j * ts + c * CHUNK, j * ts + (c + 1) * CHUNK) for c, j, _ in units]
    bc = [bc_all[r, l] for r, l in zip(rows, lanes)]
    k = [k_all[r, l] for r, l in zip(rows, lanes)]
    v = [v_all[r, l] for r, l in zip(rows, lanes)]
    q_dec = [(q_all[r, l] * jnp.exp(b)).astype(BF) for r, l, b in zip(rows, lanes, bc)]
    qk = [_dot_nt(a, (kk * jnp.exp(-b)).astype(BF)) for a, kk, b in zip(q_dec, k, bc)]
    o_intra = [_dot(jnp.where(causal, a, 0.0).astype(BF), vv.astype(BF)) for a, vv in zip(qk, v)]
    b_last = [b[CHUNK - 1:CHUNK, :] for b in bc]
    k_dec = [(kk * jnp.exp(bl - b)).astype(BF) for kk, bl, b in zip(k, b_last, bc)]
    upd = [_dot(vv.T.astype(BF), kd) for vv, kd in zip(v, k_dec)]
    decay = [jnp.exp(bl) for bl in b_last]
    per_chunk = nb * GLA_HEADS
    state = [st_s[s] for s in range(per_chunk)]
    o = []
    for u in range(len(units)):
        s = u % per_chunk
        o.append(o_intra[u] + _dot_nt(q_dec[u], state[s].astype(BF)))
        state[s] = state[s] * decay[u] + upd[u]
    for s in range(per_chunk):
        st_s[s] = state[s]
    mu = [jnp.mean(x, axis=-1, keepdims=True) for x in o]
    oc = [x - m for x, m in zip(o, mu)]
    var = [jnp.mean(x * x, axis=-1, keepdims=True) for x in oc]
    for u, ((c, j, _), l) in enumerate(zip(units, lanes)):
        on = oc[u] * lax.rsqrt(var[u] + GLA_NORM_EPS) * ng_ref[:, l]
        y_ref[j, c * CHUNK:(c + 1) * CHUNK, l] = (on * og_all[rows[u], l]).astype(y_ref.dtype)


def _gla_setup(w_q, w_k, w_v, w_gd, w_og, gB, gb, norm_g, nb, ts):
    weights = [_pad_heads(w_q, GLA_HEADS, GLA_DK).astype(BF), _pad_heads(w_k, GLA_HEADS, GLA_DK).astype(BF),
               w_v.astype(BF), jnp.pad(w_gd, ((0, 0), (0, LANES - GLA_GATE_LORA))).astype(BF), w_og.astype(BF),
               jnp.pad(_pad_heads(gB, GLA_HEADS, GLA_DK), ((0, LANES - GLA_GATE_LORA), (0, 0))).astype(BF),
               _row(_pad_heads(gb, GLA_HEADS, GLA_DK)), _row(norm_g)]
    scratch = [pltpu.VMEM((nb * GLA_HEADS, GLA_DV, LANES), F32)]
    return weights, scratch


RK_PAIRS = RK_HEADS // 2
RK_IN_W = 3 * HALF + LANES


def _rwkv_init(buf, h_s):
    _zero_history(buf)
    h_s[...] = jnp.zeros_like(h_s)


def _rwkv_body(xb, w_ref, mu_ref, w0_ref, wB_ref, a0_ref, aB_ref, gB_ref, kk_ref, ka_ref, rk_ref,
               lg_ref, lb_ref, y_ref, buf, h_s, *, nb, ts):
    rl = lax.broadcasted_iota(jnp.int32, (LANES, LANES), 0)
    cl = lax.broadcasted_iota(jnp.int32, (LANES, LANES), 1)
    same_head = (rl < RK_DH) == (cl < RK_DH)
    ones_bd = jnp.where(same_head, 1.0, 0.0).astype(BF)
    eye_l = rl == cl

    p = _dot(xb, w_ref[...])
    prev, = _with_history(buf, p, nb, ts, (1,))
    p = p + mu_ref[...] * (prev - p)
    r_all = p[:, 0:HALF]
    k = p[:, HALF:2 * HALF]
    v_all = p[:, 2 * HALF:3 * HALF]
    lora = p[:, 3 * HALF:]
    w_raw = -_softplus(-(w0_ref[...] + _bdot(jnp.tanh(lora), wB_ref[...]))) - 0.5
    lw_all = -jnp.exp(w_raw)
    a = jax.nn.sigmoid(a0_ref[...] + _bdot(lora, aB_ref[...]))
    g_all = _bdot(jax.nn.sigmoid(lora), gB_ref[...])
    kk = k * kk_ref[...]
    k2_all = k * (1.0 + (a - 1.0) * ka_ref[...])
    rkk = r_all * k2_all * rk_ref[...]
    kap_q, b_q, bon_q = [], [], []
    for q in range(RK_PAIRS):
        lanes_q = slice(q * LANES, (q + 1) * LANES)
        kkq = kk[:, lanes_q]
        nrm = jnp.sqrt(_split_right_dot(kkq * kkq, ones_bd))
        kap = kkq / jnp.maximum(nrm, 1e-12)
        kap_q.append(kap)
        b_q.append(kap * a[:, lanes_q])
        bon_q.append(_split_right_dot(rkk[:, lanes_q], ones_bd) * v_all[:, lanes_q])
    cum_all = _split_left_dot(_chunk_tril(nb * ts), lw_all)

    ri = lax.broadcasted_iota(jnp.int32, (CHUNK, LANES), 0)
    ci = jnp.bitwise_and(lax.broadcasted_iota(jnp.int32, (CHUNK, LANES), 1), CHUNK - 1)
    strict = ci < ri
    incl = ci <= ri
    eye2 = jnp.where(ci == ri, 1.0, 0.0)
    head_a = lax.broadcasted_iota(jnp.int32, (CHUNK, LANES), 1) < RK_DH

    units = [(c, j, q) for c in range(ts // CHUNK) for j in range(nb) for q in range(RK_PAIRS)]
    lanes = [slice(q * LANES, (q + 1) * LANES) for _, _, q in units]
    rows = [slice(j * ts + c * CHUNK, j * ts + (c + 1) * CHUNK) for c, j, _ in units]

    def bd(x):
        return _stack(jnp.where(head_a, x, 0.0), jnp.where(head_a, 0.0, x)).astype(BF)

    def bd_t(x):
        xt = x.T
        return jnp.where(same_head, jnp.concatenate([xt, xt], axis=1), 0.0)

    cum = [cum_all[r_, l] for r_, l in zip(rows, lanes)]
    c_last = [x[CHUNK - 1:CHUNK, :] for x in cum]
    ginv = [jnp.exp(-x) for x in cum]
    gend = [jnp.exp(cl_ - x) for cl_, x in zip(c_last, cum)]
    rt = [r_all[r_, l] * jnp.exp(x) for r_, l, x in zip(rows, lanes, cum)]
    kt = [kap_q[q][r_, :] * jnp.exp(x - lw_all[r_, l]) for (_, _, q), r_, l, x in zip(units, rows, lanes, cum)]
    k2h = [k2_all[r_, l] for r_, l in zip(rows, lanes)]
    bh = [b_q[q][r_, :] for (_, _, q), r_ in zip(units, rows)]
    lhs = [_stack(a_, b_).astype(BF) for a_, b_ in zip(kt, rt)]
    x1 = [_dot_nt(a_, bd(k_ * g_)) for a_, k_, g_ in zip(lhs, k2h, ginv)]
    x2 = [_dot_nt(a_, bd(b_ * g_)) for a_, b_, g_ in zip(lhs, bh, ginv)]
    a_kk = [jnp.where(strict, x[:CHUNK], 0.0) for x in x1]
    a_rk = [jnp.where(incl, x[CHUNK:], 0.0) for x in x1]
    a_rb = [jnp.where(incl, x[CHUNK:], 0.0).astype(BF) for x in x2]
    xp = [jnp.where(strict, -x[:CHUNK], 0.0) for x in x2]
    t_inv = [eye2 + x for x in xp]
    xp = [_dot(x.astype(BF), bd(x)) for x in xp]
    for _ in range(4):
        z = [_dot(_stack(t, x).astype(BF), bd(x)) for t, x in zip(t_inv, xp)]
        t_inv = [t + x[:CHUNK] for t, x in zip(t_inv, z)]
        xp = [x[CHUNK:] for x in z]
    t_inv = [(t + _dot(t.astype(BF), bd(x))).astype(BF) for t, x in zip(t_inv, xp)]
    kp = [bd(_dot(t, bd(k_))) for t, k_ in zip(t_inv, kt)]
    w2 = [bd(_dot(t, bd(a_))) for t, a_ in zip(t_inv, a_kk)]
    lhs2 = [_stack(a_, bd_t(b_ * g_).astype(BF)) for a_, b_, g_ in zip(a_rb, bh, gend)]
    m1 = [_dot(a_, k_) for a_, k_ in zip(lhs2, kp)]
    m2 = [_dot(a_, w_) for a_, w_ in zip(lhs2, w2)]
    r_eff = [r_ - m[:CHUNK] for r_, m in zip(rt, m1)]
    g_eff = [jnp.where(eye_l, jnp.exp(cl_), 0.0) - m[CHUNK:] for cl_, m in zip(c_last, m1)]
    p_eff = [a_ - m[:CHUNK] for a_, m in zip(a_rk, m2)]
    q_eff = [bd_t(k_ * g_) - m[CHUNK:] for k_, g_, m in zip(k2h, gend, m2)]
    o2 = [_dot(_stack(p_, q_).astype(BF), bd(v_all[r_, l])) for p_, q_, r_, l in zip(p_eff, q_eff, rows, lanes)]
    lhs3 = [_stack(r_, g_).astype(BF) for r_, g_ in zip(r_eff, g_eff)]
    per_chunk = nb * RK_PAIRS
    state = [h_s[s] for s in range(per_chunk)]
    y = []
    for u in range(len(units)):
        s = u % per_chunk
        o1 = _dot(lhs3[u], state[s].astype(BF))
        y.append(o1[:CHUNK] + o2[u][:CHUNK])
        state[s] = o1[CHUNK:] + o2[u][CHUNK:]
    for s in range(per_chunk):
        h_s[s] = state[s]
    seg = lambda x: jnp.where(head_a, jnp.sum(jnp.where(head_a, x, 0.0), axis=-1, keepdims=True),
                              jnp.sum(jnp.where(head_a, 0.0, x), axis=-1, keepdims=True))
    mu = [seg(x) * (1.0 / RK_DH) for x in y]
    yc = [x - m for x, m in zip(y, mu)]
    var = [seg(x * x) * (1.0 / RK_DH) for x in yc]
    for u, ((c, j, q), l) in enumerate(zip(units, lanes)):
        yn = yc[u] * lax.rsqrt(var[u] + RK_NORM_EPS) * lg_ref[:, l] + lb_ref[:, l]
        y_ref[j, c * CHUNK:(c + 1) * CHUNK, l] = ((yn + bon_q[q][rows[u], :]) * g_all[rows[u], l]).astype(y_ref.dtype)


def _rwkv_setup(w_rk, mu, w0, wB, a0, aB, gB, k_k, k_a, r_k, ln_g, ln_b, nb, ts):
    d0, d1 = RK_DECAY_LORA, RK_DECAY_LORA + RK_A_LORA
    wB_p = jnp.zeros((LANES, HALF), F32).at[0:d0].set(wB)
    aB_p = jnp.zeros((LANES, HALF), F32).at[d0:d1].set(aB)
    gB_p = jnp.zeros((LANES, HALF), F32).at[d1:].set(gB)
    weights = [w_rk.astype(BF), _row(mu), _row(w0), wB_p.astype(BF), _row(a0), aB_p.astype(BF), gB_p.astype(BF),
               _row(k_k), _row(k_a), _row(r_k.reshape(-1)), _row(ln_g), _row(ln_b)]
    scratch = [pltpu.VMEM((nb, ts + CARRY_ROWS, RK_IN_W), F32), pltpu.VMEM((nb * RK_PAIRS, LANES, LANES), F32)]
    return weights, scratch


def _mixer_layer_kernel(*refs, mixers, n_weights, n_scratch, nb, ts):
    (_, body_a), (_, body_b) = mixers
    na, nbw = n_weights
    x_ref = refs[0]
    w_a = refs[1:1 + na]
    w_b = refs[1 + na:1 + na + nbw]
    woa_ref, wob_ref, g_ref, b_ref, o_ref, ya_ref, yb_ref = refs[1 + na + nbw:8 + na + nbw]
    s_a = refs[8 + na + nbw:8 + na + nbw + n_scratch[0]]
    s_b = refs[8 + na + nbw + n_scratch[0]:]

    @pl.when(pl.program_id(1) == 0)
    def _():
        mixers[0][0](*s_a)
        mixers[1][0](*s_b)

    xb = x_ref[...].reshape(nb * ts, D_MODEL).astype(BF)
    body_a(xb, *w_a, ya_ref, *s_a, nb=nb, ts=ts)
    body_b(xb, *w_b, yb_ref, *s_b, nb=nb, ts=ts)
    mix = (_dot(ya_ref[...].reshape(nb * ts, HALF), woa_ref[...])
           + _dot(yb_ref[...].reshape(nb * ts, HALF), wob_ref[...]))
    x = x_ref[...].reshape(nb * ts, D_MODEL)
    o_ref[...] = _layer_norm(DEEPNORM_ALPHA * x + mix, g_ref[...], b_ref[...]).reshape(nb, ts, D_MODEL)


def _mixer_layer(x3, mixers, setups, w_out, g, b, nb, ts, name):
    bsz, seq, _ = x3.shape
    (w_a, s_a), (w_b, s_b) = setups
    tail = [w_out[:HALF].astype(BF), w_out[HALF:].astype(BF), _row(g), _row(b)]
    block = lambda width: pl.BlockSpec((nb, ts, width), lambda i, s: (i, s, 0))
    kernel = functools.partial(_mixer_layer_kernel, mixers=mixers, n_weights=(len(w_a), len(w_b)),
                               n_scratch=(len(s_a), len(s_b)), nb=nb, ts=ts)
    return pl.pallas_call(
        kernel,
        grid=(bsz // nb, seq // ts),
        in_specs=[block(D_MODEL)] + [_whole(w.shape) for w in w_a + w_b + tail],
        out_specs=block(D_MODEL),
        out_shape=jax.ShapeDtypeStruct((bsz, seq, D_MODEL), F32),
        scratch_shapes=[pltpu.VMEM((nb, ts, HALF), BF), pltpu.VMEM((nb, ts, HALF), BF)] + s_a + s_b,
        compiler_params=_params(2),
        name=name,
    )(x3, *w_a, *w_b, *tail)


_EVEN_MIXERS = ((_rglru_init, _rglru_body), (_mlstm_init, _mlstm_body))
_ODD_MIXERS = ((_rwkv_init, _rwkv_body), (_gla_init, _gla_body))


def _tile(n, pref):
    t = min(n, pref)
    while n % t:
        t //= 2
    return t


def kernel(x, ffn1_wi, ffn1_wo, ffn2_wi, ffn2_wo, ln_g, ln_b, ev_w_in, ev_w_out, rg_conv_w, rg_conv_b, rg_wa, rg_wx, rg_ba, rg_bx, rg_lambda, ml_conv_w, ml_conv_b, ml_i_bias, ml_f_bias, ml_norm_g, od_w_in, od_w_out, rk_mu, rk_w0, rk_wB, rk_a0, rk_aB, rk_gB, rk_k_k, rk_k_a, rk_r_k, rk_ln_g, rk_ln_b, gla_gB, gla_gb, gla_norm_g):
    bsz, seq, d = x.shape
    assert d == D_MODEL and seq % CHUNK == 0
    t = bsz * seq
    tm = _tile(t, FFN_ROWS)
    nb = _tile(bsz, SEQS_PER_BLOCK)
    ts = _tile(seq, ROWS_PER_BLOCK // nb)
    rk_in = 3 * HALF + RK_DECAY_LORA + RK_A_LORA + RK_G_LORA
    gq, gk = GLA_HEADS * GLA_DK, 2 * GLA_HEADS * GLA_DK
    gv, gg = gk + HALF, gk + HALF + GLA_GATE_LORA
    x2 = x.reshape(t, d)
    for l in range(DEPTH):
        x2 = _ffn_ln(x2, ffn1_wi[l].astype(BF), ffn1_wo[l].astype(BF), ln_g[l, 0], ln_b[l, 0], tm)
        x3 = x2.reshape(bsz, seq, d)
        if l % 2 == 0:
            e = l // 2
            w = ev_w_in[e]
            setups = (_rglru_setup(w[:, 0:HALF], w[:, HALF:2 * HALF], rg_conv_w[e], rg_conv_b[e], rg_wa[e], rg_wx[e],
                                   rg_ba[e], rg_bx[e], rg_lambda[e], nb, ts),
                      _mlstm_setup(w[:, 2 * HALF:4 * HALF], w[:, 4 * HALF:5 * HALF], w[:, 5 * HALF:6 * HALF],
                                   w[:, 6 * HALF:], ml_conv_w[e], ml_conv_b[e], ml_i_bias[e], ml_f_bias[e],
                                   ml_norm_g[e], nb, ts))
            x3 = _mixer_layer(x3, _EVEN_MIXERS, setups, ev_w_out[e], ln_g[l, 1], ln_b[l, 1], nb, ts, "even_mix")
        else:
            o = l // 2
            w = od_w_in[o]
            wg = w[:, rk_in:]
            setups = (_rwkv_setup(w[:, :rk_in], rk_mu[o], rk_w0[o], rk_wB[o], rk_a0[o], rk_aB[o], rk_gB[o],
                                  rk_k_k[o], rk_k_a[o], rk_r_k[o], rk_ln_g[o], rk_ln_b[o], nb, ts),
                      _gla_setup(wg[:, :gq], wg[:, gq:gk], wg[:, gk:gv], wg[:, gv:gg], wg[:, gg:],
                                 gla_gB[o], gla_gb[o], gla_norm_g[o], nb, ts))
            x3 = _mixer_layer(x3, _ODD_MIXERS, setups, od_w_out[o], ln_g[l, 1], ln_b[l, 1], nb, ts, "odd_mix")
        x2 = _ffn_ln(x3.reshape(t, d), ffn2_wi[l].astype(BF), ffn2_wo[l].astype(BF), ln_g[l, 2], ln_b[l, 2], tm)
    return x2.reshape(bsz, seq, d)
```

```python
import functools

import jax
import jax.numpy as jnp
from jax import lax
from jax.experimental import pallas as pl
from jax.experimental.pallas import tpu as pltpu

F32 = jnp.float32
BF = jnp.bfloat16

D_MODEL = 1024
DEPTH = 4
D_FF = 2816
HALF = 512
CONV_WIDTH = 4
RG_BLOCKS = 8
RG_C = 8.0
ML_HEADS = 4
ML_DH = HALF // ML_HEADS
ML_NORM_EPS = 1e-6
RK_HEADS = 8
RK_DH = HALF // RK_HEADS
RK_DECAY_LORA = 32
RK_A_LORA = 32
RK_G_LORA = 64
RK_NORM_EPS = 64e-5
GLA_HEADS = 4
GLA_DK = 64
GLA_DV = HALF // GLA_HEADS
GLA_GATE_LORA = 16
GLA_TAU = 16.0
GLA_NORM_EPS = 1e-5
DEEPNORM_ALPHA = (2.0 * DEPTH) ** 0.25
LN_EPS = 1e-5

LANES = 128
SUBLANES = 8
CHUNK = 64
CARRY_ROWS = 8
VMEM_LIMIT = 56 * 1024 * 1024
SEQS_PER_BLOCK = 4
ROWS_PER_BLOCK = 512
FFN_ROWS = 1024
FFN_PARTS = 4


def _dot(a, b):
    return jnp.dot(a, b, preferred_element_type=F32)


def _dot_nt(a, b):
    return lax.dot_general(a, b, (((1,), (1,)), ((), ())), preferred_element_type=F32)


def _bdot(a, b):
    return _dot(a.astype(BF), b.astype(BF))


def _layer_norm(r, g, b):
    mu = jnp.mean(r, axis=-1, keepdims=True)
    c = r - mu
    var = jnp.mean(c * c, axis=-1, keepdims=True)
    return c * lax.rsqrt(var + LN_EPS) * g + b


def _softplus(z):
    return jnp.maximum(z, 0.0) + jnp.log1p(jnp.exp(-jnp.abs(z)))


def _log_sigmoid(z):
    return -_softplus(-z)


def _split_hi_lo(x):
    hi = x.astype(BF)
    lo = (x - hi.astype(F32)).astype(BF)
    return hi, lo


def _split_left_dot(l_bf, x):
    hi, lo = _split_hi_lo(x)
    return _dot(l_bf, hi) + _dot(l_bf, lo)


def _split_right_dot(x, r_bf):
    hi, lo = _split_hi_lo(x)
    return _dot(hi, r_bf) + _dot(lo, r_bf)


def _chunk_tril(n):
    r = lax.broadcasted_iota(jnp.int32, (n, n), 0)
    c = lax.broadcasted_iota(jnp.int32, (n, n), 1)
    same = jnp.bitwise_and(r, -CHUNK) == jnp.bitwise_and(c, -CHUNK)
    return jnp.where((c <= r) & same, 1.0, 0.0).astype(BF)


def _linear_scan_rows(a, b, h_prev):
    n, w = a.shape
    groups = n // SUBLANES
    a = a.reshape(groups, SUBLANES, w)
    b = b.reshape(groups, SUBLANES, w)
    sub = lax.broadcasted_iota(jnp.int32, a.shape, 1)
    d = 1
    while d < SUBLANES:
        keep = sub >= d
        a_s = jnp.where(keep, pltpu.roll(a, d, axis=1), 1.0)
        b_s = jnp.where(keep, pltpu.roll(b, d, axis=1), 0.0)
        b = a * b_s + b
        a = a * a_s
        d *= 2
    out = []
    for g in range(groups):
        h = b[g] + a[g] * h_prev
        out.append(h)
        h_prev = h[SUBLANES - 1:SUBLANES, :]
    return jnp.concatenate(out, axis=0), h_prev


def _whole(shape):
    return pl.BlockSpec(shape, lambda *_: (0,) * len(shape), pipeline_mode=pl.Buffered(1))


def _params(n_axes):
    return pltpu.CompilerParams(
        dimension_semantics=("arbitrary",) * n_axes, vmem_limit_bytes=VMEM_LIMIT)


def _pad_heads(w, n_heads, dh):
    lead = w.shape[:-1]
    w = w.reshape(*lead, n_heads, dh)
    w = jnp.pad(w, [(0, 0)] * len(lead) + [(0, 0), (0, LANES - dh)])
    return w.reshape(*lead, n_heads * LANES)


def _row(v):
    return v.reshape(1, -1).astype(F32)


def _stack(top, bot):
    return jnp.concatenate([top, bot], axis=0)


def _ffn_kernel(x_ref, wi_ref, wo_ref, g_ref, b_ref, o_ref, *, parts):
    rows = x_ref.shape[0] // parts
    for i in range(parts):
        sl = slice(i * rows, (i + 1) * rows)
        x = x_ref[sl, :]
        xb = x.astype(BF)
        gate = _dot(xb, wi_ref[:, :D_FF])
        up = _dot(xb, wi_ref[:, D_FF:])
        act = (gate * jax.nn.sigmoid(gate) * up).astype(BF)
        y = _dot(act, wo_ref[...])
        o_ref[sl, :] = _layer_norm(DEEPNORM_ALPHA * x + 0.5 * y, g_ref[...], b_ref[...])


def _ffn_ln(x2, wi, wo, g, b, tm):
    t = x2.shape[0]
    return pl.pallas_call(
        functools.partial(_ffn_kernel, parts=FFN_PARTS),
        grid=(t // tm,),
        in_specs=[pl.BlockSpec((tm, D_MODEL), lambda i: (i, 0)),
                  _whole(wi.shape), _whole(wo.shape), _whole((1, D_MODEL)), _whole((1, D_MODEL))],
        out_specs=pl.BlockSpec((tm, D_MODEL), lambda i: (i, 0)),
        out_shape=jax.ShapeDtypeStruct((t, D_MODEL), F32),
        compiler_params=_params(1),
        name="ffn_ln",
    )(x2, wi, wo, _row(g), _row(b))


def _with_history(buf, cur, nb, ts, shifts):
    outs = [[] for _ in shifts]
    for j in range(nb):
        buf[j, CARRY_ROWS:CARRY_ROWS + ts, :] = cur[j * ts:(j + 1) * ts, :]
        for o, d in zip(outs, shifts):
            o.append(buf[j, CARRY_ROWS - d:CARRY_ROWS - d + ts, :])
        buf[j, 0:CARRY_ROWS, :] = buf[j, ts:ts + CARRY_ROWS, :]
    return [jnp.concatenate(o, axis=0) if nb > 1 else o[0] for o in outs]


def _causal_conv(buf, cur, cw_ref, cb_ref, nb, ts):
    out = cb_ref[...] + cw_ref[CONV_WIDTH - 1:CONV_WIDTH, :] * cur
    delayed = _with_history(buf, cur, nb, ts, range(1, CONV_WIDTH))
    for d, prev in zip(range(1, CONV_WIDTH), delayed):
        out = out + cw_ref[CONV_WIDTH - 1 - d:CONV_WIDTH - d, :] * prev
    return out


def _zero_history(buf):
    nb, _, w = buf.shape
    buf[:, 0:CARRY_ROWS, :] = jnp.zeros((nb, CARRY_ROWS, w), F32)


def _rglru_init(buf, hcar):
    _zero_history(buf)
    hcar[...] = jnp.zeros_like(hcar)


def _rglru_body(xb, wx_ref, wg_ref, cw_ref, cb_ref, wa_ref, wi_ref, ba_ref, bi_ref, lam_ref,
                y_ref, buf, hcar, *, nb, ts):
    xr = _dot(xb, wx_ref[...])
    xg = _dot(xb, wg_ref[...])
    u = _causal_conv(buf, xr, cw_ref, cb_ref, nb, ts)
    ub = u.astype(BF)
    r = jax.nn.sigmoid(_dot(ub, wa_ref[...]) + ba_ref[...])
    i = jax.nn.sigmoid(_dot(ub, wi_ref[...]) + bi_ref[...])
    log_a = -RG_C * r * _softplus(-lam_ref[...])
    a = jnp.exp(log_a)
    bb = jnp.sqrt(1.0 - a * a) * (i * u)
    gate = jax.nn.gelu(xg, approximate=True)
    for j in range(nb):
        rows = slice(j * ts, (j + 1) * ts)
        h, h_last = _linear_scan_rows(a[rows], bb[rows], hcar[j:j + 1, :])
        hcar[j:j + 1, :] = h_last
        y_ref[j] = (h * gate[rows]).astype(y_ref.dtype)


def _block_diag(w):
    n, c, d = w.shape
    eye = jnp.eye(n, dtype=w.dtype)
    return (eye[:, None, :, None] * w[:, :, None, :]).reshape(n * c, n * d)


def _rglru_setup(w_x, w_g, conv_w, conv_b, wa, wx, ba, bx, lam, nb, ts):
    weights = [w_x.astype(BF), w_g.astype(BF), conv_w.astype(F32), _row(conv_b),
               _block_diag(wa).astype(BF), _block_diag(wx).astype(BF), _row(ba), _row(bx), _row(lam)]
    scratch = [pltpu.VMEM((nb, ts + CARRY_ROWS, HALF), F32), pltpu.VMEM((max(nb, SUBLANES), HALF), F32)]
    return weights, scratch


def _mlstm_init(buf, c_s, m_s):
    _zero_history(buf)
    c_s[...] = jnp.zeros_like(c_s)
    m_s[...] = jnp.zeros_like(m_s)


def _mlstm_body(xb, wqk_ref, wv_ref, wo_ref, wif_ref, cw_ref, cb_ref, gb_ref, ng_ref,
                y_ref, buf, c_s, m_s, *, nb, ts):
    qk = _causal_conv(buf, _dot(xb, wqk_ref[...]), cw_ref, cb_ref, nb, ts)
    qk = qk * jax.nn.sigmoid(qk)
    q_all = qk[:, :HALF] * (ML_DH ** -0.5)
    k_all = qk[:, HALF:]
    v_all = _dot(xb, wv_ref[...])
    og_all = jax.nn.sigmoid(_dot(xb, wo_ref[...]))
    z = _dot(xb, wif_ref[...]) + gb_ref[...]
    fcum = _split_left_dot(_chunk_tril(nb * ts), _log_sigmoid(z))
    lane = lax.broadcasted_iota(jnp.int32, z.shape, 1)
    comb = jnp.where(lane < ML_HEADS, z, fcum)
    sel_r = lax.broadcasted_iota(jnp.int32, (LANES, 2 * ML_HEADS * LANES), 0)
    sel_c = lax.broadcasted_iota(jnp.int32, (LANES, 2 * ML_HEADS * LANES), 1)
    sel = jnp.where(jnp.bitwise_and(sel_c, -LANES) == sel_r * LANES, 1.0, 0.0).astype(BF)
    col = _split_right_dot(comb, sel)

    ri = lax.broadcasted_iota(jnp.int32, (CHUNK, CHUNK), 0)
    ci = lax.broadcasted_iota(jnp.int32, (CHUNK, CHUNK), 1)
    causal = ci <= ri
    ones = jnp.ones((CHUNK, LANES), F32)

    chains = [(j, h) for j in range(nb) for h in range(ML_HEADS)]
    state = [c_s[u] for u in range(len(chains))]
    m = [m_s[u:u + 1, :] for u in range(len(chains))]
    for c in range(ts // CHUNK):
        rows = [slice(j * ts + c * CHUNK, j * ts + (c + 1) * CHUNK) for j, _ in chains]
        lanes = [slice(h * ML_DH, (h + 1) * ML_DH) for _, h in chains]
        comb_t = [comb[j * ts + c * CHUNK:j * ts + (c + 1) * CHUNK, :].T for j in range(nb)]
        qb = [q_all[r, l].astype(BF) for r, l in zip(rows, lanes)]
        k = [k_all[r, l] for r, l in zip(rows, lanes)]
        qk_c = [_dot_nt(a, b.astype(BF)) for a, b in zip(qb, k)]
        vaug = [jnp.concatenate([v_all[r, l], ones], axis=1).astype(BF) for r, l in zip(rows, lanes)]
        qc = [_dot(a, b.astype(BF)) for a, b in zip(qb, state)]
        i_col = [col[r, h * LANES:(h + 1) * LANES] for r, (_, h) in zip(rows, chains)]
        f_col = [col[r, (ML_HEADS + h) * LANES:(ML_HEADS + h + 1) * LANES] for r, (_, h) in zip(rows, chains)]
        i_row = [comb_t[j][h:h + 1, :] for j, h in chains]
        f_row = [comb_t[j][ML_HEADS + h:ML_HEADS + h + 1, :] for j, h in chains]
        d = [jnp.where(causal, fc[:, :CHUNK] - fr + ir, -jnp.inf) for fc, fr, ir in zip(f_col, f_row, i_row)]
        inter = [f + mm for f, mm in zip(f_col, m)]
        m_t = [jnp.maximum(a, jnp.max(b, axis=-1, keepdims=True)) for a, b in zip(inter, d)]
        s = [(a * jnp.exp(b - mt[:, :CHUNK])).astype(BF) for a, b, mt in zip(qk_c, d, m_t)]
        intra = [_dot(a, b) for a, b in zip(s, vaug)]
        w_inter = [jnp.exp(it - mt) for it, mt in zip(inter, m_t)]
        num = [a[:, :ML_DH] + w * b[:, :ML_DH] for a, w, b in zip(intra, w_inter, qc)]
        den = [a[:, ML_DH:] + w * b[:, ML_DH:] for a, w, b in zip(intra, w_inter, qc)]
        f_last = [f[CHUNK - 1:CHUNK, :] for f in f_col]
        g = [fl - f + i for fl, f, i in zip(f_last, f_col, i_col)]
        m_new = [jnp.maximum(fl + mm, jnp.max(gg, axis=0, keepdims=True)) for fl, mm, gg in zip(f_last, m, g)]
        kw = [(kk * jnp.exp(gg - mn)).T.astype(BF) for kk, gg, mn in zip(k, g, m_new)]
        upd = [_dot(a, b) for a, b in zip(kw, vaug)]
        decay = [jnp.exp(fl + mm - mn) for fl, mm, mn in zip(f_last, m, m_new)]
        state = [jnp.concatenate([dc, dc], axis=1) * st + up for dc, st, up in zip(decay, state, upd)]
        m = m_new
        hh = [n / jnp.maximum(jnp.abs(dd), jnp.exp(-mt)) for n, dd, mt in zip(num, den, m_t)]
        mu = [jnp.mean(x, axis=-1, keepdims=True) for x in hh]
        hc = [x - y for x, y in zip(hh, mu)]
        var = [jnp.mean(x * x, axis=-1, keepdims=True) for x in hc]
        for u, ((j, _), l) in enumerate(zip(chains, lanes)):
            hn = hc[u] * lax.rsqrt(var[u] + ML_NORM_EPS) * ng_ref[:, l]
            y_ref[j, c * CHUNK:(c + 1) * CHUNK, l] = (hn * og_all[rows[u], l]).astype(y_ref.dtype)
    for u in range(len(chains)):
        c_s[u] = state[u]
        m_s[u:u + 1, :] = m[u]


def _mlstm_setup(w_qk, w_v, w_o, w_if, conv_w, conv_b, i_bias, f_bias, norm_g, nb, ts):
    wif = jnp.pad(w_if, ((0, 0), (0, LANES - 2 * ML_HEADS)))
    gate_bias = jnp.pad(jnp.concatenate([i_bias, f_bias]), (0, LANES - 2 * ML_HEADS))
    weights = [w_qk.astype(BF), w_v.astype(BF), w_o.astype(BF), wif.astype(BF),
               conv_w.astype(F32), _row(conv_b), _row(gate_bias), _row(norm_g)]
    scratch = [pltpu.VMEM((nb, ts + CARRY_ROWS, 2 * HALF), F32),
               pltpu.VMEM((nb * ML_HEADS, ML_DH, 2 * LANES), F32),
               pltpu.VMEM((max(nb * ML_HEADS, SUBLANES), LANES), F32)]
    return weights, scratch


def _gla_init(st_s):
    st_s[...] = jnp.zeros_like(st_s)


def _gla_body(xb, wq_ref, wk_ref, wv_ref, wgd_ref, wog_ref, gB_ref, gb_ref, ng_ref,
              y_ref, st_s, *, nb, ts):
    q_all = _dot(xb, wq_ref[...]) * (GLA_DK ** -0.5)
    k_all = _dot(xb, wk_ref[...])
    v_all = _dot(xb, wv_ref[...])
    og = _dot(xb, wog_ref[...])
    og_all = og * jax.nn.sigmoid(og)
    gd = _dot(xb, wgd_ref[...])
    log_alpha = _log_sigmoid(_bdot(gd, gB_ref[...]) + gb_ref[...]) / GLA_TAU
    bc_all = _split_left_dot(_chunk_tril(nb * ts), log_alpha)

    ri = lax.broadcasted_iota(jnp.int32, (CHUNK, CHUNK), 0)
    ci = lax.broadcasted_iota(jnp.int32, (CHUNK, CHUNK), 1)
    causal = ci <= ri

    units = [(c, j, h) for c in range(ts // CHUNK) for j in range(nb) for h in range(GLA_HEADS)]
    lanes = [slice(h * LANES, (h + 1) * LANES) for _, _, h in units]
    rows = [slice(j * ts + c * CHUNK, j * ts + (c + 1) * CHUNK) for c, j, _ in units]
    bc = [bc_all[r, l] for r, l in zip(rows, lanes)]
    k = [k_all[r, l] for r, l in zip(rows, lanes)]
    v = [v_all[r, l] for r, l in zip(rows, lanes)]
    q_dec = [(q_all[r, l] * jnp.exp(b)).astype(BF) for r, l, b in zip(rows, lanes, bc)]
    qk = [_dot_nt(a, (kk * jnp.exp(-b)).astype(BF)) for a, kk, b in zip(q_dec, k, bc)]
    o_intra = [_dot(jnp.where(causal, a, 0.0).astype(BF), vv.astype(BF)) for a, vv in zip(qk, v)]
    b_last = [b[CHUNK - 1:CHUNK, :] for b in bc]
    k_dec = [(kk * jnp.exp(bl - b)).astype(BF) for kk, bl, b in zip(k, b_last, bc)]
    upd = [_dot(vv.T.astype(BF), kd) for vv, kd in zip(v, k_dec)]
    decay = [jnp.exp(bl) for bl in b_last]
    per_chunk = nb * GLA_HEADS
    state = [st_s[s] for s in range(per_chunk)]
    o = []
    for u in range(len(units)):
        s = u % per_chunk
        o.append(o_intra[u] + _dot_nt(q_dec[u], state[s].astype(BF)))
        state[s] = state[s] * decay[u] + upd[u]
    for s in range(per_chunk):
        st_s[s] = state[s]
    mu = [jnp.mean(x, axis=-1, keepdims=True) for x in o]
    oc = [x - m for x, m in zip(o, mu)]
    var = [jnp.mean(x * x, axis=-1, keepdims=True) for x in oc]
    for u, ((c, j, _), l) in enumerate(zip(units, lanes)):
        on = oc[u] * lax.rsqrt(var[u] + GLA_NORM_EPS) * ng_ref[:, l]
        y_ref[j, c * CHUNK:(c + 1) * CHUNK, l] = (on * og_all[rows[u], l]).astype(y_ref.dtype)


def _gla_setup(w_q, w_k, w_v, w_gd, w_og, gB, gb, norm_g, nb, ts):
    weights = [_pad_heads(w_q, GLA_HEADS, GLA_DK).astype(BF), _pad_heads(w_k, GLA_HEADS, GLA_DK).astype(BF),
               w_v.astype(BF), jnp.pad(w_gd, ((0, 0), (0, LANES - GLA_GATE_LORA))).astype(BF), w_og.astype(BF),
               jnp.pad(_pad_heads(gB, GLA_HEADS, GLA_DK), ((0, LANES - GLA_GATE_LORA), (0, 0))).astype(BF),
               _row(_pad_heads(gb, GLA_HEADS, GLA_DK)), _row(norm_g)]
    scratch = [pltpu.VMEM((nb * GLA_HEADS, GLA_DV, LANES), F32)]
    return weights, scratch


RK_PAIRS = RK_HEADS // 2
RK_IN_W = 3 * HALF + LANES


def _rwkv_init(buf, h_s):
    _zero_history(buf)
    h_s[...] = jnp.zeros_like(h_s)


def _rwkv_body(xb, w_ref, mu_ref, w0_ref, wB_ref, a0_ref, aB_ref, gB_ref, kk_ref, ka_ref, rk_ref,
               lg_ref, lb_ref, y_ref, buf, h_s, *, nb, ts):
    rl = lax.broadcasted_iota(jnp.int32, (LANES, LANES), 0)
    cl = lax.broadcasted_iota(jnp.int32, (LANES, LANES), 1)
    same_head = (rl < RK_DH) == (cl < RK_DH)
    ones_bd = jnp.where(same_head, 1.0, 0.0).astype(BF)
    eye_l = rl == cl

    p = _dot(xb, w_ref[...])
    prev, = _with_history(buf, p, nb, ts, (1,))
    p = p + mu_ref[...] * (prev - p)
    r_all = p[:, 0:HALF]
    k = p[:, HALF:2 * HALF]
    v_all = p[:, 2 * HALF:3 * HALF]
    lora = p[:, 3 * HALF:]
    w_raw = -_softplus(-(w0_ref[...] + _bdot(jnp.tanh(lora), wB_ref[...]))) - 0.5
    lw_all = -jnp.exp(w_raw)
    a = jax.nn.sigmoid(a0_ref[...] + _bdot(lora, aB_ref[...]))
    g_all = _bdot(jax.nn.sigmoid(lora), gB_ref[...])
    kk = k * kk_ref[...]
    k2_all = k * (1.0 + (a - 1.0) * ka_ref[...])
    rkk = r_all * k2_all * rk_ref[...]
    kap_q, b_q, bon_q = [], [], []
    for q in range(RK_PAIRS):
        lanes_q = slice(q * LANES, (q + 1) * LANES)
        kkq = kk[:, lanes_q]
        nrm = jnp.sqrt(_split_right_dot(kkq * kkq, ones_bd))
        kap = kkq / jnp.maximum(nrm, 1e-12)
        kap_q.append(kap)
        b_q.append(kap * a[:, lanes_q])
        bon_q.append(_split_right_dot(rkk[:, lanes_q], ones_bd) * v_all[:, lanes_q])
    cum_all = _split_left_dot(_chunk_tril(nb * ts), lw_all)

    ri = lax.broadcasted_iota(jnp.int32, (CHUNK, LANES), 0)
    ci = jnp.bitwise_and(lax.broadcasted_iota(jnp.int32, (CHUNK, LANES), 1), CHUNK - 1)
    strict = ci < ri
    incl = ci <= ri
    eye2 = jnp.where(ci == ri, 1.0, 0.0)
    head_a = lax.broadcasted_iota(jnp.int32, (CHUNK, LANES), 1) < RK_DH

    units = [(c, j, q) for c in range(ts // CHUNK) for j in range(nb) for q in range(RK_PAIRS)]
    lanes = [slice(q * LANES, (q + 1) * LANES) for _, _, q in units]
    rows = [slice(j * ts + c * CHUNK, j * ts + (c + 1) * CHUNK) for c, j, _ in units]

    def bd(x):
        return _stack(jnp.where(head_a, x, 0.0), jnp.where(head_a, 0.0, x)).astype(BF)

    def bd_t(x):
        xt = x.T
        return jnp.where(same_head, jnp.concatenate([xt, xt], axis=1), 0.0)

    cum = [cum_all[r_, l] for r_, l in zip(rows, lanes)]
    c_last = [x[CHUNK - 1:CHUNK, :] for x in cum]
    ginv = [jnp.exp(-x) for x in cum]
    gend = [jnp.exp(cl_ - x) for cl_, x in zip(c_last, cum)]
    rt = [r_all[r_, l] * jnp.exp(x) for r_, l, x in zip(rows, lanes, cum)]
    kt = [kap_q[q][r_, :] * jnp.exp(x - lw_all[r_, l]) for (_, _, q), r_, l, x in zip(units, rows, lanes, cum)]
    k2h = [k2_all[r_, l] for r_, l in zip(rows, lanes)]
    bh = [b_q[q][r_, :] for (_, _, q), r_ in zip(units, rows)]
    lhs = [_stack(a_, b_).astype(BF) for a_, b_ in zip(kt, rt)]
    x1 = [_dot_nt(a_, bd(k_ * g_)) for a_, k_, g_ in zip(lhs, k2h, ginv)]
    x2 = [_dot_nt(a_, bd(b_ * g_)) for a_, b_, g_ in zip(lhs, bh, ginv)]
    a_kk = [jnp.where(strict, x[:CHUNK], 0.0) for x in x1]
    a_rk = [jnp.where(incl, x[CHUNK:], 0.0) for x in x1]
    a_rb = [jnp.where(incl, x[CHUNK:], 0.0).astype(BF) for x in x2]
    xp = [jnp.where(strict, -x[:CHUNK], 0.0) for x in x2]
    t_inv = [eye2 + x for x in xp]
    xp = [_dot(x.astype(BF), bd(x)) for x in xp]
    for _ in range(4):
        z = [_dot(_stack(t, x).astype(BF), bd(x)) for t, x in zip(t_inv, xp)]
        t_inv = [t + x[:CHUNK] for t, x in zip(t_inv, z)]
        xp = [x[CHUNK:] for x in z]
    t_inv = [(t + _dot(t.astype(BF), bd(x))).astype(BF) for t, x in zip(t_inv, xp)]
    kp = [bd(_dot(t, bd(k_))) for t, k_ in zip(t_inv, kt)]
    w2 = [bd(_dot(t, bd(a_))) for t, a_ in zip(t_inv, a_kk)]
    lhs2 = [_stack(a_, bd_t(b_ * g_).astype(BF)) for a_, b_, g_ in zip(a_rb, bh, gend)]
    m1 = [_dot(a_, k_) for a_, k_ in zip(lhs2, kp)]
    m2 = [_dot(a_, w_) for a_, w_ in zip(lhs2, w2)]
    r_eff = [r_ - m[:CHUNK] for r_, m in zip(rt, m1)]
    g_eff = [jnp.where(eye_l, jnp.exp(cl_), 0.0) - m[CHUNK:] for cl_, m in zip(c_last, m1)]
    p_eff = [a_ - m[:CHUNK] for a_, m in zip(a_rk, m2)]
    q_eff = [bd_t(k_ * g_) - m[CHUNK:] for k_, g_, m in zip(k2h, gend, m2)]
    o2 = [_dot(_stack(p_, q_).astype(BF), bd(v_all[r_, l])) for p_, q_, r_, l in zip(p_eff, q_eff, rows, lanes)]
    lhs3 = [_stack(r_, g_).astype(BF) for r_, g_ in zip(r_eff, g_eff)]
    per_chunk = nb * RK_PAIRS
    state = [h_s[s] for s in range(per_chunk)]
    y = []
    for u in range(len(units)):
        s = u % per_chunk
        o1 = _dot(lhs3[u], state[s].astype(BF))
        y.append(o1[:CHUNK] + o2[u][:CHUNK])
        state[s] = o1[CHUNK:] + o2[u][CHUNK:]
    for s in range(per_chunk):
        h_s[s] = state[s]
    seg = lambda x: jnp.where(head_a, jnp.sum(jnp.where(head_a, x, 0.0), axis=-1, keepdims=True),
                              jnp.sum(jnp.where(head_a, 0.0, x), axis=-1, keepdims=True))
    mu = [seg(x) * (1.0 / RK_DH) for x in y]
    yc = [x - m for x, m in zip(y, mu)]
    var = [seg(x * x) * (1.0 / RK_DH) for x in yc]
    for u, ((c, j, q), l) in enumerate(zip(units, lanes)):
        yn = yc[u] * lax.rsqrt(var[u] + RK_NORM_EPS) * lg_ref[:, l] + lb_ref[:, l]
        y_ref[j, c * CHUNK:(c + 1) * CHUNK, l] = ((yn + bon_q[q][rows[u], :]) * g_all[rows[u], l]).astype(y_ref.dtype)


def _rwkv_setup(w_rk, mu, w0, wB, a0, aB, gB, k_k, k_a, r_k, ln_g, ln_b, nb, ts):
    d0, d1 = RK_DECAY_LORA, RK_DECAY_LORA + RK_A_LORA
    wB_p = jnp.zeros((LANES, HALF), F32).at[0:d0].set(wB)
    aB_p = jnp.zeros((LANES, HALF), F32).at[d0:d1].set(aB)
    gB_p = jnp.zeros((LANES, HALF), F32).at[d1:].set(gB)
    weights = [w_rk.astype(BF), _row(mu), _row(w0), wB_p.astype(BF), _row(a0), aB_p.astype(BF), gB_p.astype(BF),
               _row(k_k), _row(k_a), _row(r_k.reshape(-1)), _row(ln_g), _row(ln_b)]
    scratch = [pltpu.VMEM((nb, ts + CARRY_ROWS, RK_IN_W), F32), pltpu.VMEM((nb * RK_PAIRS, LANES, LANES), F32)]
    return weights, scratch


def _mixer_layer_kernel(*refs, mixers, n_weights, n_scratch, nb, ts):
    (_, body_a), (_, body_b) = mixers
    na, nbw = n_weights
    x_ref = refs[0]
    w_a = refs[1:1 + na]
    w_b = refs[1 + na:1 + na + nbw]
    woa_ref, wob_ref, g_ref, b_ref, o_ref, ya_ref, yb_ref = refs[1 + na + nbw:8 + na + nbw]
    s_a = refs[8 + na + nbw:8 + na + nbw + n_scratch[0]]
    s_b = refs[8 + na + nbw + n_scratch[0]:]

    @pl.when(pl.program_id(1) == 0)
    def _():
        mixers[0][0](*s_a)
        mixers[1][0](*s_b)

    xb = x_ref[...].reshape(nb * ts, D_MODEL).astype(BF)
    body_a(xb, *w_a, ya_ref, *s_a, nb=nb, ts=ts)
    body_b(xb, *w_b, yb_ref, *s_b, nb=nb, ts=ts)
    mix = (_dot(ya_ref[...].reshape(nb * ts, HALF), woa_ref[...])
           + _dot(yb_ref[...].reshape(nb * ts, HALF), wob_ref[...]))
    x = x_ref[...].reshape(nb * ts, D_MODEL)
    o_ref[...] = _layer_norm(DEEPNORM_ALPHA * x + mix, g_ref[...], b_ref[...]).reshape(nb, ts, D_MODEL)


def _mixer_layer(x3, mixers, setups, w_out, g, b, nb, ts, name):
    bsz, seq, _ = x3.shape
    (w_a, s_a), (w_b, s_b) = setups
    tail = [w_out[:HALF].astype(BF), w_out[HALF:].astype(BF), _row(g), _row(b)]
    block = lambda width: pl.BlockSpec((nb, ts, width), lambda i, s: (i, s, 0))
    kernel = functools.partial(_mixer_layer_kernel, mixers=mixers, n_weights=(len(w_a), len(w_b)),
                               n_scratch=(len(s_a), len(s_b)), nb=nb, ts=ts)
    return pl.pallas_call(
        kernel,
        grid=(bsz // nb, seq // ts),
        in_specs=[block(D_MODEL)] + [_whole(w.shape) for w in w_a + w_b + tail],
        out_specs=block(D_MODEL),
        out_shape=jax.ShapeDtypeStruct((bsz, seq, D_MODEL), F32),
        scratch_shapes=[pltpu.VMEM((nb, ts, HALF), BF), pltpu.VMEM((nb, ts, HALF), BF)] + s_a + s_b,
        compiler_params=_params(2),
        name=name,
    )(x3, *w_a, *w_b, *tail)


_EVEN_MIXERS = ((_rglru_init, _rglru_body), (_mlstm_init, _mlstm_body))
_ODD_MIXERS = ((_rwkv_init, _rwkv_body), (_gla_init, _gla_body))


def _tile(n, pref):
    t = min(n, pref)
    while n % t:
        t //= 2
    return t


def kernel(x, ffn1_wi, ffn1_wo, ffn2_wi, ffn2_wo, ln_g, ln_b, ev_w_in, ev_w_out, rg_conv_w, rg_conv_b, rg_wa, rg_wx, rg_ba, rg_bx, rg_lambda, ml_conv_w, ml_conv_b, ml_i_bias, ml_f_bias, ml_norm_g, od_w_in, od_w_out, rk_mu, rk_w0, rk_wB, rk_a0, rk_aB, rk_gB, rk_k_k, rk_k_a, rk_r_k, rk_ln_g, rk_ln_b, gla_gB, gla_gb, gla_norm_g):
    bsz, seq, d = x.shape
    assert d == D_MODEL and seq % CHUNK == 0
    t = bsz * seq
    tm = _tile(t, FFN_ROWS)
    nb = _tile(bsz, SEQS_PER_BLOCK)
    ts = _tile(seq, ROWS_PER_BLOCK // nb)
    rk_in = 3 * HALF + RK_DECAY_LORA + RK_A_LORA + RK_G_LORA
    gq, gk = GLA_HEADS * GLA_DK, 2 * GLA_HEADS * GLA_DK
    gv, gg = gk + HALF, gk + HALF + GLA_GATE_LORA
    x2 = x.reshape(t, d)
    for l in range(DEPTH):
        x2 = _ffn_ln(x2, ffn1_wi[l].astype(BF), ffn1_wo[l].astype(BF), ln_g[l, 0], ln_b[l, 0], tm)
        x3 = x2.reshape(bsz, seq, d)
        if l % 2 == 0:
            e = l // 2
            w = ev_w_in[e]
            setups = (_rglru_setup(w[:, 0:HALF], w[:, HALF:2 * HALF], rg_conv_w[e], rg_conv_b[e], rg_wa[e], rg_wx[e],
                                   rg_ba[e], rg_bx[e], rg_lambda[e], nb, ts),
                      _mlstm_setup(w[:, 2 * HALF:4 * HALF], w[:, 4 * HALF:5 * HALF], w[:, 5 * HALF:6 * HALF],
                                   w[:, 6 * HALF:], ml_conv_w[e], ml_conv_b[e], ml_i_bias[e], ml_f_bias[e],
                                   ml_norm_g[e], nb, ts))
            x3 = _mixer_layer(x3, _EVEN_MIXERS, setups, ev_w_out[e], ln_g[l, 1], ln_b[l, 1], nb, ts, "even_mix")
        else:
            o = l // 2
            w = od_w_in[o]
            wg = w[:, rk_in:]
            setups = (_rwkv_setup(w[:, :rk_in], rk_mu[o], rk_w0[o], rk_wB[o], rk_a0[o], rk_aB[o], rk_gB[o],
                                  rk_k_k[o], rk_k_a[o], rk_r_k[o], rk_ln_g[o], rk_ln_b[o], nb, ts),
                      _gla_setup(wg[:, :gq], wg[:, gq:gk], wg[:, gk:gv], wg[:, gv:gg], wg[:, gg:],
                                 gla_gB[o], gla_gb[o], gla_norm_g[o], nb, ts))
            x3 = _mixer_layer(x3, _ODD_MIXERS, setups, od_w_out[o], ln_g[l, 1], ln_b[l, 1], nb, ts, "odd_mix")
        x2 = _ffn_ln(x3.reshape(t, d), ffn2_wi[l].astype(BF), ffn2_wo[l].astype(BF), ln_g[l, 2], ln_b[l, 2], tm)
    return x2.reshape(bsz, seq, d)
```

```python
import functools

import jax
import jax.numpy as jnp
from jax import lax
from jax.experimental import pallas as pl
from jax.experimental.pallas import tpu as pltpu

F32 = jnp.float32
BF = jnp.bfloat16

D_MODEL = 1024
DEPTH = 4
D_FF = 2816
HALF = 512
CONV_WIDTH = 4
RG_BLOCKS = 8
RG_C = 8.0
ML_HEADS = 4
ML_DH = HALF // ML_HEADS
ML_NORM_EPS = 1e-6
RK_HEADS = 8
RK_DH = HALF // RK_HEADS
RK_DECAY_LORA = 32
RK_A_LORA = 32
RK_G_LORA = 64
RK_NORM_EPS = 64e-5
GLA_HEADS = 4
GLA_DK = 64
GLA_DV = HALF // GLA_HEADS
GLA_GATE_LORA = 16
GLA_TAU = 16.0
GLA_NORM_EPS = 1e-5
DEEPNORM_ALPHA = (2.0 * DEPTH) ** 0.25
LN_EPS = 1e-5

LANES = 128
SUBLANES = 8
CHUNK = 64
CARRY_ROWS = 8
VMEM_LIMIT = 56 * 1024 * 1024
SEQS_PER_BLOCK = 4
ROWS_PER_BLOCK = 512
FFN_ROWS = 1024
FFN_PARTS = 4


def _dot(a, b):
    return jnp.dot(a, b, preferred_element_type=F32)


def _dot_nt(a, b):
    return lax.dot_general(a, b, (((1,), (1,)), ((), ())), preferred_element_type=F32)


def _bdot(a, b):
    return _dot(a.astype(BF), b.astype(BF))


def _layer_norm(r, g, b):
    mu = jnp.mean(r, axis=-1, keepdims=True)
    c = r - mu
    var = jnp.mean(c * c, axis=-1, keepdims=True)
    return c * lax.rsqrt(var + LN_EPS) * g + b


def _softplus(z):
    return jnp.maximum(z, 0.0) + jnp.log1p(jnp.exp(-jnp.abs(z)))


def _log_sigmoid(z):
    return -_softplus(-z)


def _split_hi_lo(x):
    hi = x.astype(BF)
    lo = (x - hi.astype(F32)).astype(BF)
    return hi, lo


def _split_left_dot(l_bf, x):
    hi, lo = _split_hi_lo(x)
    return _dot(l_bf, hi) + _dot(l_bf, lo)


def _split_right_dot(x, r_bf):
    hi, lo = _split_hi_lo(x)
    return _dot(hi, r_bf) + _dot(lo, r_bf)


def _chunk_tril(n):
    r = lax.broadcasted_iota(jnp.int32, (n, n), 0)
    c = lax.broadcasted_iota(jnp.int32, (n, n), 1)
    same = jnp.bitwise_and(r, -CHUNK) == jnp.bitwise_and(c, -CHUNK)
    return jnp.where((c <= r) & same, 1.0, 0.0).astype(BF)


MXU_K = 256


def _chunk_cumsum(x):
    n = x.shape[0]
    slab = min(n, MXU_K)
    tril = _chunk_tril(slab)
    parts = [_split_left_dot(tril, x[i:i + slab]) for i in range(0, n, slab)]
    return parts[0] if len(parts) == 1 else jnp.concatenate(parts, axis=0)


def _linear_scan_rows(a, b, h_prev):
    n, w = a.shape
    groups = n // SUBLANES
    a = a.reshape(groups, SUBLANES, w)
    b = b.reshape(groups, SUBLANES, w)
    sub = lax.broadcasted_iota(jnp.int32, a.shape, 1)
    d = 1
    while d < SUBLANES:
        keep = sub >= d
        a_s = jnp.where(keep, pltpu.roll(a, d, axis=1), 1.0)
        b_s = jnp.where(keep, pltpu.roll(b, d, axis=1), 0.0)
        b = a * b_s + b
        a = a * a_s
        d *= 2
    out = []
    for g in range(groups):
        h = b[g] + a[g] * h_prev
        out.append(h)
        h_prev = h[SUBLANES - 1:SUBLANES, :]
    return jnp.concatenate(out, axis=0), h_prev


def _whole(shape):
    return pl.BlockSpec(shape, lambda *_: (0,) * len(shape), pipeline_mode=pl.Buffered(1))


def _params(n_axes):
    return pltpu.CompilerParams(
        dimension_semantics=("arbitrary",) * n_axes, vmem_limit_bytes=VMEM_LIMIT)


def _pad_heads(w, n_heads, dh):
    lead = w.shape[:-1]
    w = w.reshape(*lead, n_heads, dh)
    w = jnp.pad(w, [(0, 0)] * len(lead) + [(0, 0), (0, LANES - dh)])
    return w.reshape(*lead, n_heads * LANES)


def _row(v):
    return v.reshape(1, -1).astype(F32)


def _stack(top, bot):
    return jnp.concatenate([top, bot], axis=0)


def _ffn_kernel(x_ref, wi_ref, wo_ref, g_ref, b_ref, o_ref, *, parts):
    rows = x_ref.shape[0] // parts
    for i in range(parts):
        sl = slice(i * rows, (i + 1) * rows)
        x = x_ref[sl, :]
        xb = x.astype(BF)
        gate = _dot(xb, wi_ref[:, :D_FF])
        up = _dot(xb, wi_ref[:, D_FF:])
        act = (gate * jax.nn.sigmoid(gate) * up).astype(BF)
        y = _dot(act, wo_ref[...])
        o_ref[sl, :] = _layer_norm(DEEPNORM_ALPHA * x + 0.5 * y, g_ref[...], b_ref[...])


def _ffn_ln(x2, wi, wo, g, b, tm):
    t = x2.shape[0]
    return pl.pallas_call(
        functools.partial(_ffn_kernel, parts=FFN_PARTS),
        grid=(t // tm,),
        in_specs=[pl.BlockSpec((tm, D_MODEL), lambda i: (i, 0)),
                  _whole(wi.shape), _whole(wo.shape), _whole((1, D_MODEL)), _whole((1, D_MODEL))],
        out_specs=pl.BlockSpec((tm, D_MODEL), lambda i: (i, 0)),
        out_shape=jax.ShapeDtypeStruct((t, D_MODEL), F32),
        compiler_params=_params(1),
        name="ffn_ln",
    )(x2, wi, wo, _row(g), _row(b))


def _with_history(buf, cur, nb, ts, shifts):
    outs = [[] for _ in shifts]
    for j in range(nb):
        buf[j, CARRY_ROWS:CARRY_ROWS + ts, :] = cur[j * ts:(j + 1) * ts, :]
        for o, d in zip(outs, shifts):
            o.append(buf[j, CARRY_ROWS - d:CARRY_ROWS - d + ts, :])
        buf[j, 0:CARRY_ROWS, :] = buf[j, ts:ts + CARRY_ROWS, :]
    return [jnp.concatenate(o, axis=0) if nb > 1 else o[0] for o in outs]


def _causal_conv(buf, cur, cw_ref, cb_ref, nb, ts):
    out = cb_ref[...] + cw_ref[CONV_WIDTH - 1:CONV_WIDTH, :] * cur
    delayed = _with_history(buf, cur, nb, ts, range(1, CONV_WIDTH))
    for d, prev in zip(range(1, CONV_WIDTH), delayed):
        out = out + cw_ref[CONV_WIDTH - 1 - d:CONV_WIDTH - d, :] * prev
    return out


def _zero_history(buf):
    nb, _, w = buf.shape
    buf[:, 0:CARRY_ROWS, :] = jnp.zeros((nb, CARRY_ROWS, w), F32)


def _rglru_init(buf, hcar):
    _zero_history(buf)
    hcar[...] = jnp.zeros_like(hcar)


def _rglru_body(xb, wx_ref, wg_ref, cw_ref, cb_ref, wa_ref, wi_ref, ba_ref, bi_ref, lam_ref,
                y_ref, buf, hcar, *, nb, ts):
    xr = _dot(xb, wx_ref[...])
    xg = _dot(xb, wg_ref[...])
    u = _causal_conv(buf, xr, cw_ref, cb_ref, nb, ts)
    ub = u.astype(BF)
    r = jax.nn.sigmoid(_dot(ub, wa_ref[...]) + ba_ref[...])
    i = jax.nn.sigmoid(_dot(ub, wi_ref[...]) + bi_ref[...])
    log_a = -RG_C * r * _softplus(-lam_ref[...])
    a = jnp.exp(log_a)
    bb = jnp.sqrt(1.0 - a * a) * (i * u)
    gate = jax.nn.gelu(xg, approximate=True)
    for j in range(nb):
        rows = slice(j * ts, (j + 1) * ts)
        h, h_last = _linear_scan_rows(a[rows], bb[rows], hcar[j:j + 1, :])
        hcar[j:j + 1, :] = h_last
        y_ref[j] = (h * gate[rows]).astype(y_ref.dtype)


def _block_diag(w):
    n, c, d = w.shape
    eye = jnp.eye(n, dtype=w.dtype)
    return (eye[:, None, :, None] * w[:, :, None, :]).reshape(n * c, n * d)


def _rglru_setup(w_x, w_g, conv_w, conv_b, wa, wx, ba, bx, lam, nb, ts):
    weights = [w_x.astype(BF), w_g.astype(BF), conv_w.astype(F32), _row(conv_b),
               _block_diag(wa).astype(BF), _block_diag(wx).astype(BF), _row(ba), _row(bx), _row(lam)]
    scratch = [pltpu.VMEM((nb, ts + CARRY_ROWS, HALF), F32), pltpu.VMEM((max(nb, SUBLANES), HALF), F32)]
    return weights, scratch


def _mlstm_init(buf, c_s, m_s):
    _zero_history(buf)
    c_s[...] = jnp.zeros_like(c_s)
    m_s[...] = jnp.zeros_like(m_s)


def _mlstm_body(xb, wqk_ref, wv_ref, wo_ref, wif_ref, cw_ref, cb_ref, gb_ref, ng_ref,
                y_ref, buf, c_s, m_s, *, nb, ts):
    qk = _causal_conv(buf, _dot(xb, wqk_ref[...]), cw_ref, cb_ref, nb, ts)
    qk = qk * jax.nn.sigmoid(qk)
    q_all = qk[:, :HALF] * (ML_DH ** -0.5)
    k_all = qk[:, HALF:]
    v_all = _dot(xb, wv_ref[...])
    og_all = jax.nn.sigmoid(_dot(xb, wo_ref[...]))
    z = _dot(xb, wif_ref[...]) + gb_ref[...]
    fcum = _chunk_cumsum(_log_sigmoid(z))
    lane = lax.broadcasted_iota(jnp.int32, z.shape, 1)
    comb = jnp.where(lane < ML_HEADS, z, fcum)
    sel_r = lax.broadcasted_iota(jnp.int32, (LANES, 2 * ML_HEADS * LANES), 0)
    sel_c = lax.broadcasted_iota(jnp.int32, (LANES, 2 * ML_HEADS * LANES), 1)
    sel = jnp.where(jnp.bitwise_and(sel_c, -LANES) == sel_r * LANES, 1.0, 0.0).astype(BF)
    col = _split_right_dot(comb, sel)

    ri = lax.broadcasted_iota(jnp.int32, (CHUNK, CHUNK), 0)
    ci = lax.broadcasted_iota(jnp.int32, (CHUNK, CHUNK), 1)
    causal = ci <= ri
    ones = jnp.ones((CHUNK, LANES), F32)

    chains = [(j, h) for j in range(nb) for h in range(ML_HEADS)]
    state = [c_s[u] for u in range(len(chains))]
    m = [m_s[u:u + 1, :] for u in range(len(chains))]
    for c in range(ts // CHUNK):
        rows = [slice(j * ts + c * CHUNK, j * ts + (c + 1) * CHUNK) for j, _ in chains]
        lanes = [slice(h * ML_DH, (h + 1) * ML_DH) for _, h in chains]
        comb_t = [comb[j * ts + c * CHUNK:j * ts + (c + 1) * CHUNK, :].T for j in range(nb)]
        qb = [q_all[r, l].astype(BF) for r, l in zip(rows, lanes)]
        k = [k_all[r, l] for r, l in zip(rows, lanes)]
        qk_c = [_dot_nt(a, b.astype(BF)) for a, b in zip(qb, k)]
        vaug = [jnp.concatenate([v_all[r, l], ones], axis=1).astype(BF) for r, l in zip(rows, lanes)]
        qc = [_dot(a, b.astype(BF)) for a, b in zip(qb, state)]
        i_col = [col[r, h * LANES:(h + 1) * LANES] for r, (_, h) in zip(rows, chains)]
        f_col = [col[r, (ML_HEADS + h) * LANES:(ML_HEADS + h + 1) * LANES] for r, (_, h) in zip(rows, chains)]
        i_row = [comb_t[j][h:h + 1, :] for j, h in chains]
        f_row = [comb_t[j][ML_HEADS + h:ML_HEADS + h + 1, :] for j, h in chains]
        d = [jnp.where(causal, fc[:, :CHUNK] - fr + ir, -jnp.inf) for fc, fr, ir in zip(f_col, f_row, i_row)]
        inter = [f + mm for f, mm in zip(f_col, m)]
        m_t = [jnp.maximum(a, jnp.max(b, axis=-1, keepdims=True)) for a, b in zip(inter, d)]
        s = [(a * jnp.exp(b - mt[:, :CHUNK])).astype(BF) for a, b, mt in zip(qk_c, d, m_t)]
        intra = [_dot(a, b) for a, b in zip(s, vaug)]
        w_inter = [jnp.exp(it - mt) for it, mt in zip(inter, m_t)]
        num = [a[:, :ML_DH] + w * b[:, :ML_DH] for a, w, b in zip(intra, w_inter, qc)]
        den = [a[:, ML_DH:] + w * b[:, ML_DH:] for a, w, b in zip(intra, w_inter, qc)]
        f_last = [f[CHUNK - 1:CHUNK, :] for f in f_col]
        g = [fl - f + i for fl, f, i in zip(f_last, f_col, i_col)]
        m_new = [jnp.maximum(fl + mm, jnp.max(gg, axis=0, keepdims=True)) for fl, mm, gg in zip(f_last, m, g)]
        kw = [(kk * jnp.exp(gg - mn)).T.astype(BF) for kk, gg, mn in zip(k, g, m_new)]
        upd = [_dot(a, b) for a, b in zip(kw, vaug)]
        decay = [jnp.exp(fl + mm - mn) for fl, mm, mn in zip(f_last, m, m_new)]
        state = [jnp.concatenate([dc, dc], axis=1) * st + up for dc, st, up in zip(decay, state, upd)]
        m = m_new
        hh = [n / jnp.maximum(jnp.abs(dd), jnp.exp(-mt)) for n, dd, mt in zip(num, den, m_t)]
        mu = [jnp.mean(x, axis=-1, keepdims=True) for x in hh]
        hc = [x - y for x, y in zip(hh, mu)]
        var = [jnp.mean(x * x, axis=-1, keepdims=True) for x in hc]
        for u, ((j, _), l) in enumerate(zip(chains, lanes)):
            hn = hc[u] * lax.rsqrt(var[u] + ML_NORM_EPS) * ng_ref[:, l]
            y_ref[j, c * CHUNK:(c + 1) * CHUNK, l] = (hn * og_all[rows[u], l]).astype(y_ref.dtype)
    for u in range(len(chains)):
        c_s[u] = state[u]
        m_s[u:u + 1, :] = m[u]


def _mlstm_setup(w_qk, w_v, w_o, w_if, conv_w, conv_b, i_bias, f_bias, norm_g, nb, ts):
    wif = jnp.pad(w_if, ((0, 0), (0, LANES - 2 * ML_HEADS)))
    gate_bias = jnp.pad(jnp.concatenate([i_bias, f_bias]), (0, LANES - 2 * ML_HEADS))
    weights = [w_qk.astype(BF), w_v.astype(BF), w_o.astype(BF), wif.astype(BF),
               conv_w.astype(F32), _row(conv_b), _row(gate_bias), _row(norm_g)]
    scratch = [pltpu.VMEM((nb, ts + CARRY_ROWS, 2 * HALF), F32),
               pltpu.VMEM((nb * ML_HEADS, ML_DH, 2 * LANES), F32),
               pltpu.VMEM((max(nb * ML_HEADS, SUBLANES), LANES), F32)]
    return weights, scratch


def _gla_init(st_s):
    st_s[...] = jnp.zeros_like(st_s)


def _gla_body(xb, wq_ref, wk_ref, wv_ref, wgd_ref, wog_ref, gB_ref, gb_ref, ng_ref,
              y_ref, st_s, *, nb, ts):
    q_all = _dot(xb, wq_ref[...]) * (GLA_DK ** -0.5)
    k_all = _dot(xb, wk_ref[...])
    v_all = _dot(xb, wv_ref[...])
    og = _dot(xb, wog_ref[...])
    og_all = og * jax.nn.sigmoid(og)
    gd = _dot(xb, wgd_ref[...])
    log_alpha = _log_sigmoid(_bdot(gd, gB_ref[...]) + gb_ref[...]) / GLA_TAU
    bc_all = _chunk_cumsum(log_alpha)

    ri = lax.broadcasted_iota(jnp.int32, (CHUNK, CHUNK), 0)
    ci = lax.broadcasted_iota(jnp.int32, (CHUNK, CHUNK), 1)
    causal = ci <= ri

    units = [(c, j, h) for c in range(ts // CHUNK) for j in range(nb) for h in range(GLA_HEADS)]
    lanes = [slice(h * LANES, (h + 1) * LANES) for _, _, h in units]
    rows = [slice(j * ts + c * CHUNK, j * ts + (c + 1) * CHUNK) for c, j, _ in units]
    bc = [bc_all[r, l] for r, l in zip(rows, lanes)]
    k = [k_all[r, l] for r, l in zip(rows, lanes)]
    v = [v_all[r, l] for r, l in zip(rows, lanes)]
    q_dec = [(q_all[r, l] * jnp.exp(b)).astype(BF) for r, l, b in zip(rows, lanes, bc)]
    qk = [_dot_nt(a, (kk * jnp.exp(-b)).astype(BF)) for a, kk, b in zip(q_dec, k, bc)]
    o_intra = [_dot(jnp.where(causal, a, 0.0).astype(BF), vv.astype(BF)) for a, vv in zip(qk, v)]
    b_last = [b[CHUNK - 1:CHUNK, :] for b in bc]
    k_dec = [(kk * jnp.exp(bl - b)).astype(BF) for kk, bl, b in zip(k, b_last, bc)]
    upd = [_dot(vv.T.astype(BF), kd) for vv, kd in zip(v, k_dec)]
    decay = [jnp.exp(bl) for bl in b_last]
    per_chunk = nb * GLA_HEADS
    state = [st_s[s] for s in range(per_chunk)]
    o = []
    for u in range(len(units)):
        s = u % per_chunk
        o.append(o_intra[u] + _dot_nt(q_dec[u], state[s].astype(BF)))
        state[s] = state[s] * decay[u] + upd[u]
    for s in range(per_chunk):
        st_s[s] = state[s]
    mu = [jnp.mean(x, axis=-1, keepdims=True) for x in o]
    oc = [x - m for x, m in zip(o, mu)]
    var = [jnp.mean(x * x, axis=-1, keepdims=True) for x in oc]
    for u, ((c, j, _), l) in enumerate(zip(units, lanes)):
        on = oc[u] * lax.rsqrt(var[u] + GLA_NORM_EPS) * ng_ref[:, l]
        y_ref[j, c * CHUNK:(c + 1) * CHUNK, l] = (on * og_all[rows[u], l]).astype(y_ref.dtype)


def _gla_setup(w_q, w_k, w_v, w_gd, w_og, gB, gb, norm_g, nb, ts):
    weights = [_pad_heads(w_q, GLA_HEADS, GLA_DK).astype(BF), _pad_heads(w_k, GLA_HEADS, GLA_DK).astype(BF),
               w_v.astype(BF), jnp.pad(w_gd, ((0, 0), (0, LANES - GLA_GATE_LORA))).astype(BF), w_og.astype(BF),
               jnp.pad(_pad_heads(gB, GLA_HEADS, GLA_DK), ((0, LANES - GLA_GATE_LORA), (0, 0))).astype(BF),
               _row(_pad_heads(gb, GLA_HEADS, GLA_DK)), _row(norm_g)]
    scratch = [pltpu.VMEM((nb * GLA_HEADS, GLA_DV, LANES), F32)]
    return weights, scratch


RK_PAIRS = RK_HEADS // 2
RK_IN_W = 3 * HALF + LANES


def _rwkv_init(buf, h_s):
    _zero_history(buf)
    h_s[...] = jnp.zeros_like(h_s)


def _rwkv_body(xb, w_ref, mu_ref, w0_ref, wB_ref, a0_ref, aB_ref, gB_ref, kk_ref, ka_ref, rk_ref,
               lg_ref, lb_ref, y_ref, buf, h_s, *, nb, ts):
    rl = lax.broadcasted_iota(jnp.int32, (LANES, LANES), 0)
    cl = lax.broadcasted_iota(jnp.int32, (LANES, LANES), 1)
    same_head = (rl < RK_DH) == (cl < RK_DH)
    ones_bd = jnp.where(same_head, 1.0, 0.0).astype(BF)
    eye_l = rl == cl

    p = _dot(xb, w_ref[...])
    prev, = _with_history(buf, p, nb, ts, (1,))
    p = p + mu_ref[...] * (prev - p)
    r_all = p[:, 0:HALF]
    k = p[:, HALF:2 * HALF]
    v_all = p[:, 2 * HALF:3 * HALF]
    lora = p[:, 3 * HALF:]
    w_raw = -_softplus(-(w0_ref[...] + _bdot(jnp.tanh(lora), wB_ref[...]))) - 0.5
    lw_all = -jnp.exp(w_raw)
    a = jax.nn.sigmoid(a0_ref[...] + _bdot(lora, aB_ref[...]))
    g_all = _bdot(jax.nn.sigmoid(lora), gB_ref[...])
    kk = k * kk_ref[...]
    k2_all = k * (1.0 + (a - 1.0) * ka_ref[...])
    rkk = r_all * k2_all * rk_ref[...]
    kap_q, b_q, bon_q = [], [], []
    for q in range(RK_PAIRS):
        lanes_q = slice(q * LANES, (q + 1) * LANES)
        kkq = kk[:, lanes_q]
        nrm = jnp.sqrt(_split_right_dot(kkq * kkq, ones_bd))
        kap = kkq / jnp.maximum(nrm, 1e-12)
        kap_q.append(kap)
        b_q.append(kap * a[:, lanes_q])
        bon_q.append(_split_right_dot(rkk[:, lanes_q], ones_bd) * v_all[:, lanes_q])
    cum_all = _chunk_cumsum(lw_all)

    ri = lax.broadcasted_iota(jnp.int32, (CHUNK, LANES), 0)
    ci = jnp.bitwise_and(lax.broadcasted_iota(jnp.int32, (CHUNK, LANES), 1), CHUNK - 1)
    strict = ci < ri
    incl = ci <= ri
    eye2 = jnp.where(ci == ri, 1.0, 0.0)
    head_a = lax.broadcasted_iota(jnp.int32, (CHUNK, LANES), 1) < RK_DH

    units = [(c, j, q) for c in range(ts // CHUNK) for j in range(nb) for q in range(RK_PAIRS)]
    lanes = [slice(q * LANES, (q + 1) * LANES) for _, _, q in units]
    rows = [slice(j * ts + c * CHUNK, j * ts + (c + 1) * CHUNK) for c, j, _ in units]

    def bd(x):
        return _stack(jnp.where(head_a, x, 0.0), jnp.where(head_a, 0.0, x)).astype(BF)

    def bd_t(x):
        xt = x.T
        return jnp.where(same_head, jnp.concatenate([xt, xt], axis=1), 0.0)

    cum = [cum_all[r_, l] for r_, l in zip(rows, lanes)]
    c_last = [x[CHUNK - 1:CHUNK, :] for x in cum]
    ginv = [jnp.exp(-x) for x in cum]
    gend = [jnp.exp(cl_ - x) for cl_, x in zip(c_last, cum)]
    rt = [r_all[r_, l] * jnp.exp(x) for r_, l, x in zip(rows, lanes, cum)]
    kt = [kap_q[q][r_, :] * jnp.exp(x - lw_all[r_, l]) for (_, _, q), r_, l, x in zip(units, rows, lanes, cum)]
    k2h = [k2_all[r_, l] for r_, l in zip(rows, lanes)]
    bh = [b_q[q][r_, :] for (_, _, q), r_ in zip(units, rows)]
    lhs = [_stack(a_, b_).astype(BF) for a_, b_ in zip(kt, rt)]
    x1 = [_dot_nt(a_, bd(k_ * g_)) for a_, k_, g_ in zip(lhs, k2h, ginv)]
    x2 = [_dot_nt(a_, bd(b_ * g_)) for a_, b_, g_ in zip(lhs, bh, ginv)]
    a_kk = [jnp.where(strict, x[:CHUNK], 0.0) for x in x1]
    a_rk = [jnp.where(incl, x[CHUNK:], 0.0) for x in x1]
    a_rb = [jnp.where(incl, x[CHUNK:], 0.0).astype(BF) for x in x2]
    xp = [jnp.where(strict, -x[:CHUNK], 0.0) for x in x2]
    t_inv = [eye2 + x for x in xp]
    xp = [_dot(x.astype(BF), bd(x)) for x in xp]
    for _ in range(4):
        z = [_dot(_stack(t, x).astype(BF), bd(x)) for t, x in zip(t_inv, xp)]
        t_inv = [t + x[:CHUNK] for t, x in zip(t_inv, z)]
        xp = [x[CHUNK:] for x in z]
    t_inv = [(t + _dot(t.astype(BF), bd(x))).astype(BF) for t, x in zip(t_inv, xp)]
    kp = [bd(_dot(t, bd(k_))) for t, k_ in zip(t_inv, kt)]
    w2 = [bd(_dot(t, bd(a_))) for t, a_ in zip(t_inv, a_kk)]
    lhs2 = [_stack(a_, bd_t(b_ * g_).astype(BF)) for a_, b_, g_ in zip(a_rb, bh, gend)]
    m1 = [_dot(a_, k_) for a_, k_ in zip(lhs2, kp)]
    m2 = [_dot(a_, w_) for a_, w_ in zip(lhs2, w2)]
    r_eff = [r_ - m[:CHUNK] for r_, m in zip(rt, m1)]
    g_eff = [jnp.where(eye_l, jnp.exp(cl_), 0.0) - m[CHUNK:] for cl_, m in zip(c_last, m1)]
    p_eff = [a_ - m[:CHUNK] for a_, m in zip(a_rk, m2)]
    q_eff = [bd_t(k_ * g_) - m[CHUNK:] for k_, g_, m in zip(k2h, gend, m2)]
    o2 = [_dot(_stack(p_, q_).astype(BF), bd(v_all[r_, l])) for p_, q_, r_, l in zip(p_eff, q_eff, rows, lanes)]
    lhs3 = [_stack(r_, g_).astype(BF) for r_, g_ in zip(r_eff, g_eff)]
    per_chunk = nb * RK_PAIRS
    state = [h_s[s] for s in range(per_chunk)]
    y = []
    for u in range(len(units)):
        s = u % per_chunk
        o1 = _dot(lhs3[u], state[s].astype(BF))
        y.append(o1[:CHUNK] + o2[u][:CHUNK])
        state[s] = o1[CHUNK:] + o2[u][CHUNK:]
    for s in range(per_chunk):
        h_s[s] = state[s]
    seg = lambda x: jnp.where(head_a, jnp.sum(jnp.where(head_a, x, 0.0), axis=-1, keepdims=True),
                              jnp.sum(jnp.where(head_a, 0.0, x), axis=-1, keepdims=True))
    mu = [seg(x) * (1.0 / RK_DH) for x in y]
    yc = [x - m for x, m in zip(y, mu)]
    var = [seg(x * x) * (1.0 / RK_DH) for x in yc]
    for u, ((c, j, q), l) in enumerate(zip(units, lanes)):
        yn = yc[u] * lax.rsqrt(var[u] + RK_NORM_EPS) * lg_ref[:, l] + lb_ref[:, l]
        y_ref[j, c * CHUNK:(c + 1) * CHUNK, l] = ((yn + bon_q[q][rows[u], :]) * g_all[rows[u], l]).astype(y_ref.dtype)


def _rwkv_setup(w_rk, mu, w0, wB, a0, aB, gB, k_k, k_a, r_k, ln_g, ln_b, nb, ts):
    d0, d1 = RK_DECAY_LORA, RK_DECAY_LORA + RK_A_LORA
    wB_p = jnp.zeros((LANES, HALF), F32).at[0:d0].set(wB)
    aB_p = jnp.zeros((LANES, HALF), F32).at[d0:d1].set(aB)
    gB_p = jnp.zeros((LANES, HALF), F32).at[d1:].set(gB)
    weights = [w_rk.astype(BF), _row(mu), _row(w0), wB_p.astype(BF), _row(a0), aB_p.astype(BF), gB_p.astype(BF),
               _row(k_k), _row(k_a), _row(r_k.reshape(-1)), _row(ln_g), _row(ln_b)]
    scratch = [pltpu.VMEM((nb, ts + CARRY_ROWS, RK_IN_W), F32), pltpu.VMEM((nb * RK_PAIRS, LANES, LANES), F32)]
    return weights, scratch


def _mixer_layer_kernel(*refs, mixers, n_weights, n_scratch, nb, ts):
    (_, body_a), (_, body_b) = mixers
    na, nbw = n_weights
    x_ref = refs[0]
    w_a = refs[1:1 + na]
    w_b = refs[1 + na:1 + na + nbw]
    woa_ref, wob_ref, g_ref, b_ref, o_ref, ya_ref, yb_ref = refs[1 + na + nbw:8 + na + nbw]
    s_a = refs[8 + na + nbw:8 + na + nbw + n_scratch[0]]
    s_b = refs[8 + na + nbw + n_scratch[0]:]

    @pl.when(pl.program_id(1) == 0)
    def _():
        mixers[0][0](*s_a)
        mixers[1][0](*s_b)

    xb = x_ref[...].reshape(nb * ts, D_MODEL).astype(BF)
    body_a(xb, *w_a, ya_ref, *s_a, nb=nb, ts=ts)
    body_b(xb, *w_b, yb_ref, *s_b, nb=nb, ts=ts)
    mix = (_dot(ya_ref[...].reshape(nb * ts, HALF), woa_ref[...])
           + _dot(yb_ref[...].reshape(nb * ts, HALF), wob_ref[...]))
    x = x_ref[...].reshape(nb * ts, D_MODEL)
    o_ref[...] = _layer_norm(DEEPNORM_ALPHA * x + mix, g_ref[...], b_ref[...]).reshape(nb, ts, D_MODEL)


def _mixer_layer(x3, mixers, setups, w_out, g, b, nb, ts, name):
    bsz, seq, _ = x3.shape
    (w_a, s_a), (w_b, s_b) = setups
    tail = [w_out[:HALF].astype(BF), w_out[HALF:].astype(BF), _row(g), _row(b)]
    block = lambda width: pl.BlockSpec((nb, ts, width), lambda i, s: (i, s, 0))
    kernel = functools.partial(_mixer_layer_kernel, mixers=mixers, n_weights=(len(w_a), len(w_b)),
                               n_scratch=(len(s_a), len(s_b)), nb=nb, ts=ts)
    return pl.pallas_call(
        kernel,
        grid=(bsz // nb, seq // ts),
        in_specs=[block(D_MODEL)] + [_whole(w.shape) for w in w_a + w_b + tail],
        out_specs=block(D_MODEL),
        out_shape=jax.ShapeDtypeStruct((bsz, seq, D_MODEL), F32),
        scratch_shapes=[pltpu.VMEM((nb, ts, HALF), BF), pltpu.VMEM((nb, ts, HALF), BF)] + s_a + s_b,
        compiler_params=_params(2),
        name=name,
    )(x3, *w_a, *w_b, *tail)


_EVEN_MIXERS = ((_rglru_init, _rglru_body), (_mlstm_init, _mlstm_body))
_ODD_MIXERS = ((_rwkv_init, _rwkv_body), (_gla_init, _gla_body))


def _tile(n, pref):
    t = min(n, pref)
    while n % t:
        t //= 2
    return t


def kernel(x, ffn1_wi, ffn1_wo, ffn2_wi, ffn2_wo, ln_g, ln_b, ev_w_in, ev_w_out, rg_conv_w, rg_conv_b, rg_wa, rg_wx, rg_ba, rg_bx, rg_lambda, ml_conv_w, ml_conv_b, ml_i_bias, ml_f_bias, ml_norm_g, od_w_in, od_w_out, rk_mu, rk_w0, rk_wB, rk_a0, rk_aB, rk_gB, rk_k_k, rk_k_a, rk_r_k, rk_ln_g, rk_ln_b, gla_gB, gla_gb, gla_norm_g):
    bsz, seq, d = x.shape
    assert d == D_MODEL and seq % CHUNK == 0
    t = bsz * seq
    tm = _tile(t, FFN_ROWS)
    nb = _tile(bsz, SEQS_PER_BLOCK)
    ts = _tile(seq, ROWS_PER_BLOCK // nb)
    rk_in = 3 * HALF + RK_DECAY_LORA + RK_A_LORA + RK_G_LORA
    gq, gk = GLA_HEADS * GLA_DK, 2 * GLA_HEADS * GLA_DK
    gv, gg = gk + HALF, gk + HALF + GLA_GATE_LORA
    x2 = x.reshape(t, d)
    for l in range(DEPTH):
        x2 = _ffn_ln(x2, ffn1_wi[l].astype(BF), ffn1_wo[l].astype(BF), ln_g[l, 0], ln_b[l, 0], tm)
        x3 = x2.reshape(bsz, seq, d)
        if l % 2 == 0:
            e = l // 2
            w = ev_w_in[e]
            setups = (_rglru_setup(w[:, 0:HALF], w[:, HALF:2 * HALF], rg_conv_w[e], rg_conv_b[e], rg_wa[e], rg_wx[e],
                                   rg_ba[e], rg_bx[e], rg_lambda[e], nb, ts),
                      _mlstm_setup(w[:, 2 * HALF:4 * HALF], w[:, 4 * HALF:5 * HALF], w[:, 5 * HALF:6 * HALF],
                                   w[:, 6 * HALF:], ml_conv_w[e], ml_conv_b[e], ml_i_bias[e], ml_f_bias[e],
                                   ml_norm_g[e], nb, ts))
            x3 = _mixer_layer(x3, _EVEN_MIXERS, setups, ev_w_out[e], ln_g[l, 1], ln_b[l, 1], nb, ts, "even_mix")
        else:
            o = l // 2
            w = od_w_in[o]
            wg = w[:, rk_in:]
            setups = (_rwkv_setup(w[:, :rk_in], rk_mu[o], rk_w0[o], rk_wB[o], rk_a0[o], rk_aB[o], rk_gB[o],
                                  rk_k_k[o], rk_k_a[o], rk_r_k[o], rk_ln_g[o], rk_ln_b[o], nb, ts),
                      _gla_setup(wg[:, :gq], wg[:, gq:gk], wg[:, gk:gv], wg[:, gv:gg], wg[:, gg:],
                                 gla_gB[o], gla_gb[o], gla_norm_g[o], nb, ts))
            x3 = _mixer_layer(x3, _ODD_MIXERS, setups, od_w_out[o], ln_g[l, 1], ln_b[l, 1], nb, ts, "odd_mix")
        x2 = _ffn_ln(x3.reshape(t, d), ffn2_wi[l].astype(BF), ffn2_wo[l].astype(BF), ln_g[l, 2], ln_b[l, 2], tm)
    return x2.reshape(bsz, seq, d)
```
